```python
import jax, jax.numpy as jnp
from jax import lax
import numpy as np

D_MODEL = 2048
BATCH = 1
SEQ = 8192
DEPTH = 4

HEAD_DIM = 128
GROUPS = 4
BRANCH_W = GROUPS * HEAD_DIM
N_BRANCH = 4
CONV_W = 3
CHUNK = 128
POOL_WINDOWS = (2, 4, 8, 16)
Q_BLOCK = 128
PEER_HEADS = 8
PEER_KEYS = 128
PEER_N = PEER_KEYS * PEER_KEYS
PEER_DKEY = 256
PEER_HALF = PEER_DKEY // 2
PEER_TOPK = 16
TOKEN_BLOCK = 128
EPS = 1e-6

OFF_A = 0
OFF_B = OFF_A + 3 * BRANCH_W
OFF_C = OFF_B + 2 * BRANCH_W
OFF_D = OFF_C + BRANCH_W
OFF_G = OFF_D + 3 * BRANCH_W + GROUPS
IN_COLS = OFF_G + N_BRANCH * D_MODEL

kernel_name = "hybrid_conv_sgu_pool_fox_peer"


def rmsnorm(x, g):
    xf = x.astype(jnp.float32)
    y = xf * lax.rsqrt(jnp.mean(xf * xf, axis=-1, keepdims=True) + EPS)
    return (y * g.astype(jnp.float32)).astype(x.dtype)


def short_conv_mixer(cols, conv_w):
    b, c, h = jnp.split(cols, 3, axis=-1)
    z = c * h
    kern = conv_w[:, None, :].astype(z.dtype)
    y = lax.conv_general_dilated(z, kern, window_strides=(1,), padding=((CONV_W - 1, 0),),
                                 dimension_numbers=('NWC', 'WIO', 'NWC'),
                                 feature_group_count=BRANCH_W)
    return b * y


def sgu_mixer(cols, norm_g, w_s, bias):
    zc = jax.nn.gelu(cols)
    u, v = jnp.split(zc, 2, axis=-1)
    v = rmsnorm(v, norm_g)
    bsz, s, _ = v.shape
    v = v.reshape(bsz, s // CHUNK, CHUNK, GROUPS, HEAD_DIM)
    mask = jnp.tril(jnp.ones((CHUNK, CHUNK), dtype=bool))
    w = jnp.where(mask[None], w_s, 0).astype(v.dtype)
    sv = jnp.einsum('gts,bcsgd->bctgd', w, v) + bias.T[:, :, None].astype(v.dtype)
    return u * sv.reshape(bsz, s, BRANCH_W)


def pool_mixer(p, pool_w, scale):
    bsz, s, _ = p.shape
    pf = p.astype(jnp.float32)
    cs = jnp.concatenate([jnp.zeros((bsz, 1, BRANCH_W), jnp.float32), jnp.cumsum(pf, axis=1)], axis=1)
    t = jnp.arange(1, s + 1, dtype=jnp.float32)
    outs = []
    for g, w in enumerate(POOL_WINDOWS):
        sl = slice(g * HEAD_DIM, (g + 1) * HEAD_DIM)
        csg = cs[..., sl]
        lag = jnp.concatenate([jnp.zeros((bsz, w - 1, HEAD_DIM), jnp.float32), csg[:, :s - w + 1]], axis=1)
        mean = (csg[:, 1:] - lag) / jnp.minimum(t, float(w))[None, :, None]
        outs.append(mean - pf[..., sl])
    pooled = jnp.stack(outs, axis=2).astype(p.dtype)
    y = jnp.einsum('bsgi,gio->bsgo', pooled, pool_w)
    return y.reshape(bsz, s, BRANCH_W) * scale


def forgetting_attention(cols, forget_b):
    bsz, s, _ = cols.shape
    q = cols[..., 0:BRANCH_W].reshape(bsz, s, GROUPS, HEAD_DIM)
    k = cols[..., BRANCH_W:2 * BRANCH_W].reshape(bsz, s, GROUPS, HEAD_DIM)
    v = cols[..., 2 * BRANCH_W:3 * BRANCH_W].reshape(bsz, s, GROUPS, HEAD_DIM)
    f_logit = cols[..., 3 * BRANCH_W:].astype(jnp.float32) + forget_b.astype(jnp.float32)
    F = jnp.cumsum(jax.nn.log_sigmoid(f_logit), axis=1)
    Fk = jnp.transpose(F, (0, 2, 1))
    nblk = s // Q_BLOCK
    qb = q.reshape(bsz, nblk, Q_BLOCK, GROUPS, HEAD_DIM).transpose(1, 0, 2, 3, 4)
    Fb = F.reshape(bsz, nblk, Q_BLOCK, GROUPS).transpose(1, 0, 3, 2)
    key_pos = jnp.arange(s)
    scale = HEAD_DIM ** -0.5

    def block(args):
        q_i, F_i, i = args
        qpos = i * Q_BLOCK + jnp.arange(Q_BLOCK)
        sc = jnp.einsum('bqhd,bkhd->bhqk', q_i, k).astype(jnp.float32) * scale
        sc = sc + F_i[..., None] - Fk[:, :, None, :]
        sc = jnp.where((qpos[:, None] >= key_pos[None, :])[None, None], sc, -jnp.inf)
        pr = jax.nn.softmax(sc, axis=-1).astype(v.dtype)
        return jnp.einsum('bhqk,bkhd->bqhd', pr, v)

    o = lax.map(block, (qb, Fb, jnp.arange(nblk)))
    return o.transpose(1, 0, 2, 3, 4).reshape(bsz, s, BRANCH_W)


def peer_ffn(h, wq, keys, u_tab, v_tab):
    bsz, s, d = h.shape
    q = (h @ wq).reshape(bsz, s, PEER_HEADS, 2, PEER_HALF)
    sc = jnp.einsum('bshpc,hpnc->bshpn', q, keys).astype(jnp.float32)
    vals, idx = lax.top_k(sc, PEER_TOPK)
    cand = vals[..., 0, :, None] + vals[..., 1, None, :]
    cand_idx = idx[..., 0, :, None] * PEER_KEYS + idx[..., 1, None, :]
    kk = PEER_TOPK * PEER_TOPK
    top_vals, top_pos = lax.top_k(cand.reshape(bsz, s, PEER_HEADS, kk), PEER_TOPK)
    expert = jnp.take_along_axis(cand_idx.reshape(bsz, s, PEER_HEADS, kk), top_pos, axis=-1)
    gate = jax.nn.softmax(top_vals, axis=-1).astype(h.dtype)
    nblk = (bsz * s) // TOKEN_BLOCK
    hb = h.reshape(nblk, TOKEN_BLOCK, d)
    eb = expert.reshape(nblk, TOKEN_BLOCK, PEER_HEADS, PEER_TOPK)
    gb = gate.reshape(nblk, TOKEN_BLOCK, PEER_HEADS, PEER_TOPK)

    def block(args):
        h_i, e_i, g_i = args
        u = u_tab[e_i]
        a = g_i * jax.nn.gelu(jnp.einsum('td,thkd->thk', h_i, u))
        return jnp.einsum('thk,thkd->td', a, v_tab[e_i])

    out = lax.map(block, (hb, eb, gb))
    return out.reshape(bsz, s, d)


def setup_inputs(seed: int = 0) -> dict:
    key = jax.random.key(seed)
    ks = jax.random.split(key, 24)
    f32 = jnp.float32
    nrm = lambda k, shape, sc: jax.random.normal(k, shape, f32) * sc
    D = D_MODEL
    x = nrm(ks[0], (BATCH, SEQ, D), 1.0)
    norm1_g = 1.0 + nrm(ks[1], (DEPTH, D), 0.05)
    w_mix = nrm(ks[2], (DEPTH, D, OFF_D + 3 * BRANCH_W), D ** -0.5)
    w_fg = nrm(ks[3], (DEPTH, D, GROUPS), 0.1 * D ** -0.5)
    w_gate = nrm(ks[4], (DEPTH, D, N_BRANCH * D), D ** -0.5)
    w_in = jnp.concatenate([w_mix, w_fg, w_gate], axis=-1)
    conv_w = nrm(ks[5], (DEPTH, CONV_W, BRANCH_W), CONV_W ** -0.5)
    sgu_norm_g = 1.0 + nrm(ks[6], (DEPTH, BRANCH_W), 0.05)
    sgu_w = nrm(ks[7], (DEPTH, GROUPS, CHUNK, CHUNK), CHUNK ** -0.5)
    sgu_b = 1.0 + nrm(ks[8], (DEPTH, GROUPS, CHUNK), 0.1)
    pool_w = nrm(ks[9], (DEPTH, GROUPS, HEAD_DIM, HEAD_DIM), HEAD_DIM ** -0.5)
    pool_scale = 1.0 + nrm(ks[10], (DEPTH, BRANCH_W), 0.1)
    forget_b = 2.0 + nrm(ks[11], (DEPTH, GROUPS), 0.5)
    w_branch = nrm(ks[12], (DEPTH, N_BRANCH, BRANCH_W, D), BRANCH_W ** -0.5)
    w_out = nrm(ks[13], (DEPTH, D, D), D ** -0.5)
    norm2_g = 1.0 + nrm(ks[14], (DEPTH, D), 0.05)
    peer_wq = nrm(ks[15], (DEPTH, D, PEER_HEADS * PEER_DKEY), D ** -0.5)
    peer_keys = nrm(ks[16], (DEPTH, PEER_HEADS, 2, PEER_KEYS, PEER_HALF), PEER_HALF ** -0.5)
    peer_u = nrm(ks[17], (DEPTH, PEER_N, D), D ** -0.5)
    peer_v = nrm(ks[18], (DEPTH, PEER_N, D), (PEER_HEADS * PEER_TOPK) ** -0.5)
    final_g = 1.0 + nrm(ks[19], (D,), 0.05)
    return {"x": x, "norm1_g": norm1_g, "w_in": w_in, "conv_w": conv_w,
            "sgu_norm_g": sgu_norm_g, "sgu_w": sgu_w, "sgu_b": sgu_b,
            "pool_w": pool_w, "pool_scale": pool_scale, "forget_b": forget_b,
            "w_branch": w_branch, "w_out": w_out, "norm2_g": norm2_g,
            "peer_wq": peer_wq, "peer_keys": peer_keys, "peer_u": peer_u,
            "peer_v": peer_v, "final_g": final_g}


def reference(x, norm1_g, w_in, conv_w, sgu_norm_g, sgu_w, sgu_b, pool_w, pool_scale,
              forget_b, w_branch, w_out, norm2_g, peer_wq, peer_keys, peer_u, peer_v, final_g):
    bsz, s, d = x.shape
    h = x
    for l in range(DEPTH):
        xn = rmsnorm(h, norm1_g[l])
        z = xn @ w_in[l]
        oa = short_conv_mixer(z[..., OFF_A:OFF_B], conv_w[l])
        ob = sgu_mixer(z[..., OFF_B:OFF_C], sgu_norm_g[l], sgu_w[l], sgu_b[l])
        oc = pool_mixer(z[..., OFF_C:OFF_D], pool_w[l], pool_scale[l])
        od = forgetting_attention(z[..., OFF_D:OFF_G], forget_b[l])
        branches = jnp.stack([oa, ob, oc, od], axis=2)
        gates = jax.nn.sigmoid(z[..., OFF_G:].reshape(bsz, s, N_BRANCH, d))
        y = jnp.einsum('bsnm,nmd->bsnd', branches, w_branch[l])
        merged = jnp.einsum('bsnd,bsnd->bsd', gates, y)
        h = h + merged @ w_out[l]
        h = h + peer_ffn(rmsnorm(h, norm2_g[l]), peer_wq[l], peer_keys[l], peer_u[l], peer_v[l])
    return rmsnorm(h, final_g)
```

```python
import functools
import math

import numpy as np
import jax
import jax.numpy as jnp
from jax import lax
from jax.experimental import pallas as pl
from jax.experimental.pallas import tpu as pltpu

F32 = jnp.float32
BF16 = jnp.bfloat16

HEAD_DIM = 128
GROUPS = 4
BRANCH_W = GROUPS * HEAD_DIM
N_BRANCH = 4
CONV_W = 3
CHUNK = 128
POOL_WINDOWS = (2, 4, 8, 16)
PEER_HEADS = 8
PEER_KEYS = 128
PEER_HALF = 128
PEER_TOPK = 16
EPS = 1e-6

OFF_B = 3 * BRANCH_W
OFF_C = OFF_B + 2 * BRANCH_W
OFF_D = OFF_C + BRANCH_W
N_MIX = OFF_D + 3 * BRANCH_W
HIST = 16
FG_PAD = 128

V7X_VMEM_LIMIT_BYTES = 56 * 1024 * 1024
NEG_BIG = -1e30


def _params(*sem):
    return pltpu.CompilerParams(dimension_semantics=sem, vmem_limit_bytes=V7X_VMEM_LIMIT_BYTES)


def _gelu_tanh(x):
    return 0.5 * x * (1.0 + jnp.tanh(math.sqrt(2.0 / math.pi) * (x + 0.044715 * (x * x * x))))


def _rmsnorm_rows(x, g):
    return x * lax.rsqrt(jnp.mean(x * x, axis=-1, keepdims=True) + EPS) * g


def _resident(shape, index_map):
    return pl.BlockSpec(shape, index_map, pipeline_mode=pl.Buffered(1))


def _inproj_kernel(h_ref, g_ref, w_ref, wfg_ref, z_ref, fg_ref, xn_ref):
    @pl.when(pl.program_id(1) == 0)
    def _():
        xn = _rmsnorm_rows(h_ref[...], g_ref[...]).astype(BF16)
        xn_ref[...] = xn
        fg_ref[...] = jnp.dot(xn, wfg_ref[...], preferred_element_type=F32)

    z_ref[...] = jnp.dot(xn_ref[...], w_ref[...], preferred_element_type=F32)


def _inproj(h, g, w_mix, w_fg, tm, tn):
    s, d = h.shape
    n = w_mix.shape[1]
    return pl.pallas_call(
        _inproj_kernel,
        out_shape=(jax.ShapeDtypeStruct((s, n), F32),
                   jax.ShapeDtypeStruct((s, FG_PAD), F32),
                   jax.ShapeDtypeStruct((s, d), BF16)),
        grid=(s // tm, n // tn),
        in_specs=[pl.BlockSpec((tm, d), lambda i, j: (i, 0)),
                  pl.BlockSpec((1, d), lambda i, j: (0, 0)),
                  pl.BlockSpec((d, tn), lambda i, j: (0, j)),
                  pl.BlockSpec((d, FG_PAD), lambda i, j: (0, 0))],
        out_specs=(pl.BlockSpec((tm, tn), lambda i, j: (i, j)),
                   pl.BlockSpec((tm, FG_PAD), lambda i, j: (i, 0)),
                   pl.BlockSpec((tm, d), lambda i, j: (i, 0))),
        compiler_params=_params("parallel", "arbitrary"),
        name="inproj",
    )(h, g, w_mix, w_fg)


def _gates_kernel(xn_ref, w_ref, o_ref):
    z = jnp.dot(xn_ref[...], w_ref[...], preferred_element_type=F32)
    o_ref[...] = (1.0 / (1.0 + jnp.exp(-z))).astype(o_ref.dtype)


def _gates(xn, w_gate, tm, tn):
    s, d = xn.shape
    n = w_gate.shape[1]
    return pl.pallas_call(
        _gates_kernel,
        out_shape=jax.ShapeDtypeStruct((s, n), BF16),
        grid=(s // tm, n // tn),
        in_specs=[pl.BlockSpec((tm, d), lambda i, j: (i, 0)),
                  pl.BlockSpec((d, tn), lambda i, j: (0, j))],
        out_specs=pl.BlockSpec((tm, tn), lambda i, j: (i, j)),
        compiler_params=_params("parallel", "arbitrary"),
        name="gates",
    )(xn, w_gate)


CUM_BLOCK = 256


def _fcum_kernel(x_ref, b_ref, tri_ref, o_ref):
    s = x_ref.shape[1]
    carry = jnp.zeros((x_ref.shape[0], 1), F32)
    for c in range(s // CUM_BLOCK):
        sl = slice(c * CUM_BLOCK, (c + 1) * CUM_BLOCK)
        x = x_ref[:, sl] + b_ref[...]
        ls = jnp.minimum(x, 0.0) - jnp.log1p(jnp.exp(-jnp.abs(x)))
        y = jnp.dot(ls, tri_ref[...], precision=lax.Precision.HIGHEST,
                    preferred_element_type=F32) + carry
        o_ref[:, sl] = y
        carry = y[:, CUM_BLOCK - 1:CUM_BLOCK]


def _fcum(f_t, b_col):
    tri = jnp.asarray(np.triu(np.ones((CUM_BLOCK, CUM_BLOCK), np.float32)))
    return pl.pallas_call(
        _fcum_kernel,
        out_shape=jax.ShapeDtypeStruct(f_t.shape, F32),
        compiler_params=pltpu.CompilerParams(vmem_limit_bytes=V7X_VMEM_LIMIT_BYTES),
        name="fcum",
    )(f_t, b_col, tri)


def _attn_kernel(qi_ref, kj_ref, q_ref, k_ref, v_ref, fq_ref, fk_ref, o_ref, m_ref, l_ref, acc_ref,
                 *, tq, tk, scale):
    p = pl.program_id(1)
    qi = qi_ref[p]
    kj = kj_ref[p]

    @pl.when(kj == 0)
    def _():
        m_ref[...] = jnp.full(m_ref.shape, NEG_BIG, F32)
        l_ref[...] = jnp.zeros(l_ref.shape, F32)
        acc_ref[...] = jnp.zeros(acc_ref.shape, F32)

    def step(masked):
        q = q_ref[...].astype(BF16)
        k = k_ref[...].astype(BF16)
        s = lax.dot_general(q, k, (((1,), (1,)), ((), ())), preferred_element_type=F32) * scale
        fq = fq_ref[0]
        s = s + (fq[:, 0:1] - fk_ref[0])
        if masked:
            rows = qi * tq + lax.broadcasted_iota(jnp.int32, (tq, tk), 0)
            cols = kj * tk + lax.broadcasted_iota(jnp.int32, (tq, tk), 1)
            s = jnp.where(rows >= cols, s, NEG_BIG)
        m_prev = m_ref[...]
        m_new = jnp.maximum(m_prev, jnp.max(s, axis=-1, keepdims=True))
        alpha = jnp.exp(m_prev - m_new)
        pr = jnp.exp(s - m_new)
        l_ref[...] = alpha * l_ref[...] + jnp.sum(pr, axis=-1, keepdims=True)
        acc_ref[...] = alpha * acc_ref[...] + jnp.dot(
            pr.astype(BF16), v_ref[...].astype(BF16), preferred_element_type=F32)
        m_ref[...] = m_new

    last = (qi * tq + tq - 1) // tk
    first_masked = (qi * tq) // tk

    @pl.when(kj < first_masked)
    def _():
        step(False)

    @pl.when(kj >= first_masked)
    def _():
        step(True)

    @pl.when(kj == last)
    def _():
        o_ref[...] = (acc_ref[...] / l_ref[...]).astype(o_ref.dtype)


def _attention(zmix, f_rows, tq, tk):
    s = zmix.shape[0]
    nq = s // tq
    pairs = [(i, j) for i in range(nq) for j in range((i * tq + tq - 1) // tk + 1)]
    qi = jnp.asarray(np.array([p[0] for p in pairs], np.int32))
    kj = jnp.asarray(np.array([p[1] for p in pairs], np.int32))
    qcol = OFF_D // HEAD_DIM
    kcol = qcol + GROUPS
    vcol = kcol + GROUPS
    grid_spec = pltpu.PrefetchScalarGridSpec(
        num_scalar_prefetch=2,
        grid=(GROUPS, len(pairs)),
        in_specs=[
            pl.BlockSpec((tq, HEAD_DIM), lambda h, p, qi, kj: (qi[p], qcol + h)),
            pl.BlockSpec((tk, HEAD_DIM), lambda h, p, qi, kj: (kj[p], kcol + h)),
            pl.BlockSpec((tk, HEAD_DIM), lambda h, p, qi, kj: (kj[p], vcol + h)),
            pl.BlockSpec((1, 1, tq), lambda h, p, qi, kj: (h, 0, qi[p])),
            pl.BlockSpec((1, 1, tk), lambda h, p, qi, kj: (h, 0, kj[p])),
        ],
        out_specs=pl.BlockSpec((tq, HEAD_DIM), lambda h, p, qi, kj: (qi[p], h)),
        scratch_shapes=[pltpu.VMEM((tq, 1), F32), pltpu.VMEM((tq, 1), F32),
                        pltpu.VMEM((tq, HEAD_DIM), F32)],
    )
    return pl.pallas_call(
        functools.partial(_attn_kernel, tq=tq, tk=tk, scale=HEAD_DIM ** -0.5),
        out_shape=jax.ShapeDtypeStruct((s, BRANCH_W), BF16),
        grid_spec=grid_spec,
        compiler_params=_params("parallel", "arbitrary"),
        name="fox_attention",
    )(qi, kj, zmix, zmix, zmix, f_rows, f_rows)


def _mix_kernel(z_ref, zp_ref, cw_ref, ng_ref, ws_ref, sb_ref, pw_ref, ps_ref, o_ref, ext_ref, *, tb):
    i = pl.program_id(0)
    bw = BRANCH_W

    prev = zp_ref[...]
    prev = jnp.where(i > 0, prev, jnp.zeros_like(prev))
    ext_ref[0:HIST, 0:bw] = prev[:, bw:2 * bw] * prev[:, 2 * bw:3 * bw]
    ext_ref[HIST:HIST + tb, 0:bw] = z_ref[:, bw:2 * bw] * z_ref[:, 2 * bw:3 * bw]
    ext_ref[0:HIST, bw:2 * bw] = prev[:, OFF_C:OFF_C + bw]
    ext_ref[HIST:HIST + tb, bw:2 * bw] = z_ref[:, OFF_C:OFF_C + bw]
    conv = cw_ref[CONV_W - 1:CONV_W, :] * ext_ref[HIST:HIST + tb, 0:bw]
    for lag in range(1, CONV_W):
        conv = conv + cw_ref[CONV_W - 1 - lag:CONV_W - lag, :] * ext_ref[HIST - lag:HIST - lag + tb, 0:bw]
    o_ref[:, 0:bw] = (z_ref[:, 0:bw] * conv).astype(o_ref.dtype)

    u = _gelu_tanh(z_ref[:, OFF_B:OFF_B + bw])
    v = _gelu_tanh(z_ref[:, OFF_B + bw:OFF_B + 2 * bw])
    v = _rmsnorm_rows(v, ng_ref[...]).astype(BF16)
    for c in range(tb // CHUNK):
        rs = slice(c * CHUNK, (c + 1) * CHUNK)
        for g in range(GROUPS):
            cs = slice(g * HEAD_DIM, (g + 1) * HEAD_DIM)
            sv = jnp.dot(ws_ref[g], v[rs, cs], preferred_element_type=F32) + sb_ref[g]
            o_ref[rs, bw + g * HEAD_DIM:bw + (g + 1) * HEAD_DIM] = (u[rs, cs] * sv).astype(o_ref.dtype)

    t = (i * tb + 1 + lax.broadcasted_iota(jnp.int32, (tb, 1), 0)).astype(F32)
    for g, w in enumerate(POOL_WINDOWS):
        cs = slice(bw + g * HEAD_DIM, bw + (g + 1) * HEAD_DIM)
        cur = ext_ref[HIST:HIST + tb, cs]
        tot = cur
        for lag in range(1, w):
            tot = tot + ext_ref[HIST - lag:HIST - lag + tb, cs]
        pooled = (tot / jnp.minimum(t, float(w)) - cur).astype(BF16)
        y = jnp.dot(pooled, pw_ref[g], preferred_element_type=F32)
        y = y * ps_ref[:, g * HEAD_DIM:(g + 1) * HEAD_DIM]
        o_ref[:, 2 * bw + g * HEAD_DIM:2 * bw + (g + 1) * HEAD_DIM] = y.astype(o_ref.dtype)


def _mixers(zmix, conv_w, sgu_norm_g, sgu_w_tril, sgu_bias_b, pool_w, pool_scale, tb):
    s = zmix.shape[0]
    ncol = OFF_D
    hist_blocks = tb // HIST
    const2 = lambda i: (0, 0)
    const3 = lambda i: (0, 0, 0)
    return pl.pallas_call(
        functools.partial(_mix_kernel, tb=tb),
        out_shape=jax.ShapeDtypeStruct((s, 3 * BRANCH_W), BF16),
        grid=(s // tb,),
        in_specs=[pl.BlockSpec((tb, ncol), lambda i: (i, 0)),
                  pl.BlockSpec((HIST, ncol), lambda i: (jnp.maximum(i * hist_blocks - 1, 0), 0)),
                  pl.BlockSpec((CONV_W, BRANCH_W), const2),
                  pl.BlockSpec((1, BRANCH_W), const2),
                  pl.BlockSpec((GROUPS, CHUNK, CHUNK), const3),
                  pl.BlockSpec((GROUPS, CHUNK, HEAD_DIM), const3),
                  pl.BlockSpec((GROUPS, HEAD_DIM, HEAD_DIM), const3),
                  pl.BlockSpec((1, BRANCH_W), const2)],
        out_specs=pl.BlockSpec((tb, 3 * BRANCH_W), lambda i: (i, 0)),
        scratch_shapes=[pltpu.VMEM((HIST + tb, 2 * BRANCH_W), F32)],
        compiler_params=_params("parallel"),
        name="mixers",
    )(zmix, zmix, conv_w, sgu_norm_g, sgu_w_tril, sgu_bias_b, pool_w, pool_scale)


def _merge_kernel(abc_ref, od_ref, gates_ref, wb_ref, o_ref):
    d = o_ref.shape[1]
    acc = None
    for n in range(N_BRANCH):
        br = od_ref[...] if n == N_BRANCH - 1 else abc_ref[:, n * BRANCH_W:(n + 1) * BRANCH_W]
        y = jnp.dot(br, wb_ref[n], preferred_element_type=F32)
        term = gates_ref[:, n * d:(n + 1) * d].astype(F32) * y
        acc = term if acc is None else acc + term
    o_ref[...] = acc.astype(o_ref.dtype)


def _merge(abc, od, gates, w_branch, tm):
    s = abc.shape[0]
    d = w_branch.shape[2]
    return pl.pallas_call(
        _merge_kernel,
        out_shape=jax.ShapeDtypeStruct((s, d), BF16),
        grid=(s // tm,),
        in_specs=[pl.BlockSpec((tm, 3 * BRANCH_W), lambda i: (i, 0)),
                  pl.BlockSpec((tm, BRANCH_W), lambda i: (i, 0)),
                  pl.BlockSpec((tm, N_BRANCH * d), lambda i: (i, 0)),
                  _resident((N_BRANCH, BRANCH_W, d), lambda i: (0, 0, 0))],
        out_specs=pl.BlockSpec((tm, d), lambda i: (i, 0)),
        compiler_params=_params("parallel"),
        name="merge",
    )(abc, od, gates, w_branch)


def _outproj_kernel(m_ref, w_ref, h_ref, g_ref, h1_ref, hn_ref):
    h1 = h_ref[...] + jnp.dot(m_ref[...], w_ref[...], preferred_element_type=F32)
    h1_ref[...] = h1
    hn_ref[...] = _rmsnorm_rows(h1, g_ref[...]).astype(hn_ref.dtype)


def _outproj(merged, w_out, h, g2, tm):
    s, d = h.shape
    return pl.pallas_call(
        _outproj_kernel,
        out_shape=(jax.ShapeDtypeStruct((s, d), F32), jax.ShapeDtypeStruct((s, d), BF16)),
        grid=(s // tm,),
        in_specs=[pl.BlockSpec((tm, d), lambda i: (i, 0)),
                  _resident((d, d), lambda i: (0, 0)),
                  pl.BlockSpec((tm, d), lambda i: (i, 0)),
                  pl.BlockSpec((1, d), lambda i: (0, 0))],
        out_specs=(pl.BlockSpec((tm, d), lambda i: (i, 0)),
                   pl.BlockSpec((tm, d), lambda i: (i, 0))),
        compiler_params=_params("parallel"),
        name="outproj",
    )(merged, w_out, h, g2)


N_CAND = 2 * PEER_TOPK + 7 * 8


def _top16(s, sub_iota):
    t = s.shape[1]
    row16 = lax.broadcasted_iota(jnp.int32, (PEER_TOPK, t), 0)
    rank = jnp.full(s.shape, float(PEER_TOPK), F32)
    vals = jnp.zeros((PEER_TOPK, t), F32)
    work = s
    for r in range(PEER_TOPK):
        m = jnp.max(work, axis=0, keepdims=True)
        first = jnp.min(jnp.where(work == m, sub_iota, float(PEER_KEYS)), axis=0, keepdims=True)
        hit = sub_iota == first
        rank = jnp.where(hit, float(r), rank)
        work = jnp.where(hit, -jnp.inf, work)
        vals = jnp.where(row16 == r, m, vals)
    return rank, vals


def _retrieve_kernel(hn_ref, wq_ref, keys_ref, r1_ref, e1_ref, q_ref, c_ref):
    t = hn_ref.shape[0]
    qf = jnp.dot(hn_ref[...], wq_ref[...], preferred_element_type=F32).astype(BF16)
    sub_iota = lax.broadcasted_iota(jnp.int32, (PEER_KEYS, t), 0).astype(F32)
    row8 = lax.broadcasted_iota(jnp.int32, (8, t), 0)
    row16 = lax.broadcasted_iota(jnp.int32, (PEER_TOPK, t), 0)
    pos = jnp.concatenate(
        [row16.astype(F32)]
        + [(row8 + 16 * p).astype(F32) for p in range(1, 8)]
        + [((row8 + 8) * 16).astype(F32)], axis=0)

    for h in range(PEER_HEADS):
        sc = []
        for half in range(2):
            col = (h * 2 + half) * PEER_HALF
            sc.append(lax.dot_general(keys_ref[h, half], qf[:, col:col + PEER_HALF],
                                      (((1,), (1,)), ((), ())), preferred_element_type=F32))
        rank0, a = _top16(sc[0], sub_iota)
        rank1, b = _top16(sc[1], sub_iota)
        ea = jnp.exp(a - a[0:1])
        eb = jnp.exp(b - b[0:1])

        def pair(x, y, op):
            return jnp.concatenate(
                [op(x[0:1], y)] + [op(x[p:p + 1], y[0:8]) for p in range(1, 8)] + [op(x[8:16], y[0:1])],
                axis=0)

        cand = pair(a, b, jnp.add)
        wgt = pair(ea, eb, jnp.multiply)
        sel = jnp.zeros(cand.shape, F32)
        work = cand
        for _ in range(PEER_TOPK):
            m = jnp.max(work, axis=0, keepdims=True)
            first = jnp.min(jnp.where(work == m, pos, 1e9), axis=0, keepdims=True)
            hit = pos == first
            sel = jnp.where(hit, 1.0, sel)
            work = jnp.where(hit, -jnp.inf, work)
        z = jnp.sum(sel * wgt, axis=0, keepdims=True)
        cnt_lo = jnp.zeros((8, t), F32)
        cnt_lo = jnp.where(row8 == 0, jnp.sum(sel[0:16], axis=0, keepdims=True), cnt_lo)
        for p in range(1, 8):
            cnt_lo = jnp.where(row8 == p, jnp.sum(sel[8 + 8 * p:16 + 8 * p], axis=0, keepdims=True), cnt_lo)
        cnt = jnp.concatenate([cnt_lo, sel[72:80]], axis=0)
        qmap = jnp.zeros((PEER_KEYS, t), F32)
        for r in range(PEER_TOPK):
            qmap = jnp.where(rank0 == float(r), cnt[r:r + 1], qmap)
        r1_ref[h] = rank1
        e1_ref[h] = jnp.exp(sc[1] - b[0:1])
        q_ref[h] = qmap
        c_ref[h] = jnp.exp(sc[0] - a[0:1]) / z


def _retrieve(hn, wq, keys, tt):
    s, d = hn.shape
    shape = jax.ShapeDtypeStruct((PEER_HEADS, PEER_KEYS, s), F32)
    out_spec = pl.BlockSpec((PEER_HEADS, PEER_KEYS, tt), lambda i: (0, 0, i))
    return pl.pallas_call(
        _retrieve_kernel,
        out_shape=(shape, shape, shape, shape),
        grid=(s // tt,),
        in_specs=[pl.BlockSpec((tt, d), lambda i: (i, 0)),
                  _resident(wq.shape, lambda i: (0, 0)),
                  _resident(keys.shape, lambda i: (0, 0, 0, 0))],
        out_specs=(out_spec, out_spec, out_spec, out_spec),
        compiler_params=_params("parallel"),
        name="peer_retrieve",
    )(hn, wq, keys)


def _experts_kernel(hnt_ref, u_ref, v_ref, r1_ref, e1_ref, q_ref, c_ref, h_ref, o_ref, *, eb):
    e = pl.program_id(1)

    @pl.when(e == 0)
    def _():
        o_ref[...] = h_ref[...]

    act = _gelu_tanh(jnp.dot(u_ref[...], hnt_ref[...], preferred_element_type=F32))
    n_i = eb // PEER_KEYS
    blocks = []
    for ii in range(n_i):
        i = e * n_i + ii
        w = None
        for h in range(PEER_HEADS):
            gate = jnp.where(r1_ref[h] < q_ref[h, pl.ds(i, 1), :], e1_ref[h] * c_ref[h, pl.ds(i, 1), :], 0.0)
            w = gate if w is None else w + gate
        blocks.append(w * act[ii * PEER_KEYS:(ii + 1) * PEER_KEYS])
    a_t = jnp.concatenate(blocks, axis=0).astype(BF16)
    o_ref[...] += lax.dot_general(a_t, v_ref[...], (((0,), (0,)), ((), ())), preferred_element_type=F32)


def _experts(hn_t, u_tab, v_tab, r1, e1, q, c, h, tt, eb):
    d, s = hn_t.shape
    n_exp = u_tab.shape[0]
    sel_spec = pl.BlockSpec((PEER_HEADS, PEER_KEYS, tt), lambda i, e: (0, 0, i))
    return pl.pallas_call(
        functools.partial(_experts_kernel, eb=eb),
        out_shape=jax.ShapeDtypeStruct((s, d), F32),
        grid=(s // tt, n_exp // eb),
        in_specs=[pl.BlockSpec((d, tt), lambda i, e: (0, i)),
                  pl.BlockSpec((eb, d), lambda i, e: (e, 0)),
                  pl.BlockSpec((eb, d), lambda i, e: (e, 0)),
                  sel_spec, sel_spec, sel_spec, sel_spec,
                  pl.BlockSpec((tt, d), lambda i, e: (i, 0))],
        out_specs=pl.BlockSpec((tt, d), lambda i, e: (i, 0)),
        compiler_params=_params("parallel", "arbitrary"),
        name="peer_experts",
    )(hn_t, u_tab, v_tab, r1, e1, q, c, h)


def _final_kernel(h_ref, g_ref, o_ref):
    o_ref[...] = _rmsnorm_rows(h_ref[...], g_ref[...])


def _final_norm(h, g, tm):
    s, d = h.shape
    return pl.pallas_call(
        _final_kernel,
        out_shape=jax.ShapeDtypeStruct((s, d), F32),
        grid=(s // tm,),
        in_specs=[pl.BlockSpec((tm, d), lambda i: (i, 0)), pl.BlockSpec((1, d), lambda i: (0, 0))],
        out_specs=pl.BlockSpec((tm, d), lambda i: (i, 0)),
        compiler_params=_params("parallel"),
        name="final_norm",
    )(h, g)


def _tile(s, want):
    t = min(s, want)
    assert s % t == 0
    return t


def _layer(h, p):
    s, d = h.shape
    zmix, fg, xn = _inproj(h, p["norm1_g"], p["w_mix"], p["w_fg"], _tile(s, 1024), 512)
    gates = _gates(xn, p["w_gate"], _tile(s, 1024), 1024)

    f_t = jnp.transpose(fg[:, :8])
    f_cum = _fcum(f_t, p["forget_b"])
    f_rows = f_cum[:GROUPS].reshape(GROUPS, 1, s)
    od = _attention(zmix, f_rows, _tile(s, 512), _tile(s, 512))

    abc = _mixers(zmix, p["conv_w"], p["sgu_norm_g"], p["sgu_w"], p["sgu_b"], p["pool_w"],
                  p["pool_scale"], _tile(s, 512))
    merged = _merge(abc, od, gates, p["w_branch"], _tile(s, 512))
    h1, hn = _outproj(merged, p["w_out"], h, p["norm2_g"], _tile(s, 512))

    r1, e1, q, c = _retrieve(hn, p["peer_wq"], p["peer_keys"], _tile(s, 512))
    return _experts(jnp.transpose(hn), p["peer_u"], p["peer_v"], r1, e1, q, c, h1, _tile(s, 512), 512)


def kernel(x, norm1_g, w_in, conv_w, sgu_norm_g, sgu_w, sgu_b, pool_w, pool_scale, forget_b,
           w_branch, w_out, norm2_g, peer_wq, peer_keys, peer_u, peer_v, final_g):
    bsz, s, d = x.shape
    depth = w_in.shape[0]
    off_g = N_MIX + GROUPS
    assert w_in.shape[2] == off_g + N_BRANCH * d
    tril = jnp.tril(jnp.ones((CHUNK, CHUNK), dtype=bool))

    outs = []
    for b in range(bsz):
        h = x[b]
        for l in range(depth):
            p = {
                "norm1_g": norm1_g[l][None, :],
                "w_mix": w_in[l, :, :N_MIX].astype(BF16),
                "w_fg": jnp.pad(w_in[l, :, N_MIX:off_g], ((0, 0), (0, FG_PAD - GROUPS))).astype(BF16),
                "w_gate": w_in[l, :, off_g:].astype(BF16),
                "conv_w": conv_w[l],
                "sgu_norm_g": sgu_norm_g[l][None, :],
                "sgu_w": jnp.where(tril[None], sgu_w[l], 0.0).astype(BF16),
                "sgu_b": jnp.broadcast_to(sgu_b[l][:, :, None], (GROUPS, CHUNK, HEAD_DIM)),
                "pool_w": pool_w[l].astype(BF16),
                "pool_scale": pool_scale[l][None, :],
                "forget_b": jnp.pad(forget_b[l], (0, 8 - GROUPS))[:, None],
                "w_branch": w_branch[l].astype(BF16),
                "w_out": w_out[l].astype(BF16),
                "norm2_g": norm2_g[l][None, :],
                "peer_wq": peer_wq[l].astype(BF16),
                "peer_keys": peer_keys[l].astype(BF16),
                "peer_u": peer_u[l].astype(BF16),
                "peer_v": peer_v[l].astype(BF16),
            }
            h = _layer(h, p)
        outs.append(_final_norm(h, final_g[None, :], _tile(s, 512)))
    return jnp.stack(outs, axis=0)
```

```python
import functools
import math

import numpy as np
import jax
import jax.numpy as jnp
from jax import lax
from jax.experimental import pallas as pl
from jax.experimental.pallas import tpu as pltpu

F32 = jnp.float32
BF16 = jnp.bfloat16

HEAD_DIM = 128
GROUPS = 4
BRANCH_W = GROUPS * HEAD_DIM
N_BRANCH = 4
CONV_W = 3
CHUNK = 128
POOL_WINDOWS = (2, 4, 8, 16)
PEER_HEADS = 8
PEER_KEYS = 128
PEER_HALF = 128
PEER_TOPK = 16
EPS = 1e-6

OFF_B = 3 * BRANCH_W
OFF_C = OFF_B + 2 * BRANCH_W
OFF_D = OFF_C + BRANCH_W
N_MIX = OFF_D + 3 * BRANCH_W
HIST = 16
FG_PAD = 128

V7X_VMEM_LIMIT_BYTES = 56 * 1024 * 1024
NEG_BIG = -1e30
LOG2E = math.log2(math.e)


def _params(*sem):
    return pltpu.CompilerParams(dimension_semantics=sem, vmem_limit_bytes=V7X_VMEM_LIMIT_BYTES)


def _gelu_tanh(x):
    return 0.5 * x * (1.0 + jnp.tanh(math.sqrt(2.0 / math.pi) * (x + 0.044715 * (x * x * x))))


def _rmsnorm_rows(x, g):
    return x * lax.rsqrt(jnp.mean(x * x, axis=-1, keepdims=True) + EPS) * g


def _resident(shape, index_map):
    return pl.BlockSpec(shape, index_map, pipeline_mode=pl.Buffered(1))


def _inproj_kernel(*refs, has_delta):
    if has_delta:
        h_ref, dl_ref, g_ref, w_ref, wfg_ref, z_ref, fg_ref, xn_ref, hs_ref = refs
    else:
        h_ref, g_ref, w_ref, wfg_ref, z_ref, fg_ref, xn_ref = refs

    @pl.when(pl.program_id(1) == 0)
    def _():
        h = h_ref[...]
        if has_delta:
            h = h + dl_ref[...]
            hs_ref[...] = h
        xn = _rmsnorm_rows(h, g_ref[...]).astype(BF16)
        xn_ref[...] = xn
        fg_ref[...] = jnp.dot(xn, wfg_ref[...], preferred_element_type=F32)

    z_ref[...] = jnp.dot(xn_ref[...], w_ref[...], preferred_element_type=F32)


def _inproj(h, delta, g, w_mix, w_fg, tm, tn):
    s, d = h.shape
    n = w_mix.shape[1]
    has_delta = delta is not None
    row_spec = pl.BlockSpec((tm, d), lambda i, j: (i, 0))
    out_shape = [jax.ShapeDtypeStruct((s, n), F32),
                 jax.ShapeDtypeStruct((s, FG_PAD), F32),
                 jax.ShapeDtypeStruct((s, d), BF16)]
    out_specs = [pl.BlockSpec((tm, tn), lambda i, j: (i, j)),
                 pl.BlockSpec((tm, FG_PAD), lambda i, j: (i, 0)),
                 row_spec]
    if has_delta:
        out_shape.append(jax.ShapeDtypeStruct((s, d), F32))
        out_specs.append(row_spec)
    outs = pl.pallas_call(
        functools.partial(_inproj_kernel, has_delta=has_delta),
        out_shape=tuple(out_shape),
        grid=(s // tm, n // tn),
        in_specs=[row_spec] * (2 if has_delta else 1) + [
            pl.BlockSpec((1, d), lambda i, j: (0, 0)),
            pl.BlockSpec((d, tn), lambda i, j: (0, j)),
            pl.BlockSpec((d, FG_PAD), lambda i, j: (0, 0))],
        out_specs=tuple(out_specs),
        compiler_params=_params("parallel", "arbitrary"),
        name="inproj",
    )(*((h, delta) if has_delta else (h,)), g, w_mix, w_fg)
    return outs if has_delta else (*outs, h)


def _gates_kernel(xn_ref, w_ref, o_ref):
    z = jnp.dot(xn_ref[...], w_ref[...], preferred_element_type=F32)
    o_ref[...] = (1.0 / (1.0 + jnp.exp(-z))).astype(o_ref.dtype)


def _gates(xn, w_gate, tm, tn):
    s, d = xn.shape
    n = w_gate.shape[1]
    return pl.pallas_call(
        _gates_kernel,
        out_shape=jax.ShapeDtypeStruct((s, n), BF16),
        grid=(s // tm, n // tn),
        in_specs=[pl.BlockSpec((tm, d), lambda i, j: (i, 0)),
                  pl.BlockSpec((d, tn), lambda i, j: (0, j))],
        out_specs=pl.BlockSpec((tm, tn), lambda i, j: (i, j)),
        compiler_params=_params("parallel", "arbitrary"),
        name="gates",
    )(xn, w_gate)


CUM_BLOCK = 256


def _fcum_kernel(x_ref, b_ref, tri_ref, o_ref):
    s = x_ref.shape[1]
    carry = jnp.zeros((x_ref.shape[0], 1), F32)
    for c in range(s // CUM_BLOCK):
        sl = slice(c * CUM_BLOCK, (c + 1) * CUM_BLOCK)
        x = x_ref[:, sl] + b_ref[...]
        ls = jnp.minimum(x, 0.0) - jnp.log1p(jnp.exp(-jnp.abs(x)))
        y = jnp.dot(ls, tri_ref[...], precision=lax.Precision.HIGHEST,
                    preferred_element_type=F32) + carry
        o_ref[:, sl] = y
        carry = y[:, CUM_BLOCK - 1:CUM_BLOCK]


def _fcum(f_t, b_col):
    tri = jnp.asarray(np.triu(np.ones((CUM_BLOCK, CUM_BLOCK), np.float32)))
    return pl.pallas_call(
        _fcum_kernel,
        out_shape=jax.ShapeDtypeStruct(f_t.shape, F32),
        compiler_params=pltpu.CompilerParams(vmem_limit_bytes=V7X_VMEM_LIMIT_BYTES),
        name="fcum",
    )(f_t, b_col, tri)


def _attn_kernel(qi_ref, kj_ref, q_ref, k_ref, v_ref, fq_ref, fk_ref, o_ref, m_ref, l_ref, acc_ref,
                 *, tq, tk, scale):
    p = pl.program_id(0)
    qi = qi_ref[p]
    kj = kj_ref[p]

    @pl.when(kj == 0)
    def _():
        m_ref[...] = jnp.full(m_ref.shape, NEG_BIG, F32)
        l_ref[...] = jnp.zeros(l_ref.shape, F32)
        acc_ref[...] = jnp.zeros(acc_ref.shape, F32)

    def step(masked):
        if masked:
            rows = qi * tq + lax.broadcasted_iota(jnp.int32, (tq, tk), 0)
            cols = kj * tk + lax.broadcasted_iota(jnp.int32, (tq, tk), 1)
            keep = rows >= cols
        ones = jnp.ones((tk, HEAD_DIM), BF16)
        for h in range(GROUPS):
            cs = slice(h * HEAD_DIM, (h + 1) * HEAD_DIM)
            q = q_ref[:, cs].astype(BF16)
            k = k_ref[:, cs].astype(BF16)
            s = lax.dot_general(q, k, (((1,), (1,)), ((), ())), preferred_element_type=F32) * (scale * LOG2E)
            fq = fq_ref[h]
            s = s + (fq[:, 0:1] - fk_ref[h]) * LOG2E
            if masked:
                s = jnp.where(keep, s, NEG_BIG)
            m_prev = m_ref[h]
            m_new = jnp.maximum(m_prev, jnp.max(s, axis=-1, keepdims=True))
            alpha = jnp.exp2(m_prev - m_new)
            pr = jnp.exp2(s - pltpu.repeat(m_new, tk // HEAD_DIM, axis=1)).astype(BF16)
            pv = jnp.dot(pr, jnp.concatenate([v_ref[:, cs].astype(BF16), ones], axis=1),
                         preferred_element_type=F32)
            l_ref[h] = alpha * l_ref[h] + pv[:, HEAD_DIM:]
            acc_ref[:, cs] = alpha * acc_ref[:, cs] + pv[:, :HEAD_DIM]
            m_ref[h] = m_new

    last = (qi * tq + tq - 1) // tk
    first_masked = (qi * tq) // tk

    @pl.when(kj < first_masked)
    def _():
        step(False)

    @pl.when(kj >= first_masked)
    def _():
        step(True)

    @pl.when(kj == last)
    def _():
        for h in range(GROUPS):
            cs = slice(h * HEAD_DIM, (h + 1) * HEAD_DIM)
            o_ref[:, cs] = (acc_ref[:, cs] / l_ref[h]).astype(o_ref.dtype)


def _attention(zmix, f_rows, tq, tk):
    s = zmix.shape[0]
    nq = s // tq
    pairs = [(i, j) for i in range(nq) for j in range((i * tq + tq - 1) // tk + 1)]
    qi = jnp.asarray(np.array([p[0] for p in pairs], np.int32))
    kj = jnp.asarray(np.array([p[1] for p in pairs], np.int32))
    qcol = OFF_D // BRANCH_W
    grid_spec = pltpu.PrefetchScalarGridSpec(
        num_scalar_prefetch=2,
        grid=(len(pairs),),
        in_specs=[
            pl.BlockSpec((tq, BRANCH_W), lambda p, qi, kj: (qi[p], qcol)),
            pl.BlockSpec((tk, BRANCH_W), lambda p, qi, kj: (kj[p], qcol + 1)),
            pl.BlockSpec((tk, BRANCH_W), lambda p, qi, kj: (kj[p], qcol + 2)),
            pl.BlockSpec((GROUPS, 1, tq), lambda p, qi, kj: (0, 0, qi[p])),
            pl.BlockSpec((GROUPS, 1, tk), lambda p, qi, kj: (0, 0, kj[p])),
        ],
        out_specs=pl.BlockSpec((tq, BRANCH_W), lambda p, qi, kj: (qi[p], 0)),
        scratch_shapes=[pltpu.VMEM((GROUPS, tq, HEAD_DIM), F32), pltpu.VMEM((GROUPS, tq, HEAD_DIM), F32),
                        pltpu.VMEM((tq, BRANCH_W), F32)],
    )
    return pl.pallas_call(
        functools.partial(_attn_kernel, tq=tq, tk=tk, scale=HEAD_DIM ** -0.5),
        out_shape=jax.ShapeDtypeStruct((s, BRANCH_W), BF16),
        grid_spec=grid_spec,
        compiler_params=_params("arbitrary"),
        name="fox_attention",
    )(qi, kj, zmix, zmix, zmix, f_rows, f_rows)


def _mix_kernel(z_ref, zp_ref, cw_ref, ng_ref, ws_ref, sb_ref, pw_ref, ps_ref, o_ref, ext_ref, *, tb):
    i = pl.program_id(0)
    bw = BRANCH_W

    prev = zp_ref[...]
    prev = jnp.where(i > 0, prev, jnp.zeros_like(prev))
    ext_ref[0:HIST, 0:bw] = prev[:, bw:2 * bw] * prev[:, 2 * bw:3 * bw]
    ext_ref[HIST:HIST + tb, 0:bw] = z_ref[:, bw:2 * bw] * z_ref[:, 2 * bw:3 * bw]
    ext_ref[0:HIST, bw:2 * bw] = prev[:, OFF_C:OFF_C + bw]
    ext_ref[HIST:HIST + tb, bw:2 * bw] = z_ref[:, OFF_C:OFF_C + bw]
    conv = cw_ref[CONV_W - 1:CONV_W, :] * ext_ref[HIST:HIST + tb, 0:bw]
    for lag in range(1, CONV_W):
        conv = conv + cw_ref[CONV_W - 1 - lag:CONV_W - lag, :] * ext_ref[HIST - lag:HIST - lag + tb, 0:bw]
    o_ref[:, 0:bw] = (z_ref[:, 0:bw] * conv).astype(o_ref.dtype)

    u = _gelu_tanh(z_ref[:, OFF_B:OFF_B + bw])
    v = _gelu_tanh(z_ref[:, OFF_B + bw:OFF_B + 2 * bw])
    v = _rmsnorm_rows(v, ng_ref[...]).astype(BF16)
    for c in range(tb // CHUNK):
        rs = slice(c * CHUNK, (c + 1) * CHUNK)
        for g in range(GROUPS):
            cs = slice(g * HEAD_DIM, (g + 1) * HEAD_DIM)
            sv = jnp.dot(ws_ref[g], v[rs, cs], preferred_element_type=F32) + sb_ref[g]
            o_ref[rs, bw + g * HEAD_DIM:bw + (g + 1) * HEAD_DIM] = (u[rs, cs] * sv).astype(o_ref.dtype)

    t = (i * tb + 1 + lax.broadcasted_iota(jnp.int32, (tb, 1), 0)).astype(F32)
    for g, w in enumerate(POOL_WINDOWS):
        cs = slice(bw + g * HEAD_DIM, bw + (g + 1) * HEAD_DIM)
        cur = ext_ref[HIST:HIST + tb, cs]
        tot = cur
        for lag in range(1, w):
            tot = tot + ext_ref[HIST - lag:HIST - lag + tb, cs]
        pooled = (tot / jnp.minimum(t, float(w)) - cur).astype(BF16)
        y = jnp.dot(pooled, pw_ref[g], preferred_element_type=F32)
        y = y * ps_ref[:, g * HEAD_DIM:(g + 1) * HEAD_DIM]
        o_ref[:, 2 * bw + g * HEAD_DIM:2 * bw + (g + 1) * HEAD_DIM] = y.astype(o_ref.dtype)


def _mixers(zmix, conv_w, sgu_norm_g, sgu_w_tril, sgu_bias_b, pool_w, pool_scale, tb):
    s = zmix.shape[0]
    ncol = OFF_D
    hist_blocks = tb // HIST
    const2 = lambda i: (0, 0)
    const3 = lambda i: (0, 0, 0)
    return pl.pallas_call(
        functools.partial(_mix_kernel, tb=tb),
        out_shape=jax.ShapeDtypeStruct((s, 3 * BRANCH_W), BF16),
        grid=(s // tb,),
        in_specs=[pl.BlockSpec((tb, ncol), lambda i: (i, 0)),
                  pl.BlockSpec((HIST, ncol), lambda i: (jnp.maximum(i * hist_blocks - 1, 0), 0)),
                  pl.BlockSpec((CONV_W, BRANCH_W), const2),
                  pl.BlockSpec((1, BRANCH_W), const2),
                  pl.BlockSpec((GROUPS, CHUNK, CHUNK), const3),
                  pl.BlockSpec((GROUPS, CHUNK, HEAD_DIM), const3),
                  pl.BlockSpec((GROUPS, HEAD_DIM, HEAD_DIM), const3),
                  pl.BlockSpec((1, BRANCH_W), const2)],
        out_specs=pl.BlockSpec((tb, 3 * BRANCH_W), lambda i: (i, 0)),
        scratch_shapes=[pltpu.VMEM((HIST + tb, 2 * BRANCH_W), F32)],
        compiler_params=_params("parallel"),
        name="mixers",
    )(zmix, zmix, conv_w, sgu_norm_g, sgu_w_tril, sgu_bias_b, pool_w, pool_scale)


def _merge_kernel(abc_ref, od_ref, gates_ref, wb_ref, o_ref):
    d = o_ref.shape[1]
    acc = None
    for n in range(N_BRANCH):
        br = od_ref[...] if n == N_BRANCH - 1 else abc_ref[:, n * BRANCH_W:(n + 1) * BRANCH_W]
        y = jnp.dot(br, wb_ref[n], preferred_element_type=F32)
        term = gates_ref[:, n * d:(n + 1) * d].astype(F32) * y
        acc = term if acc is None else acc + term
    o_ref[...] = acc.astype(o_ref.dtype)


def _merge(abc, od, gates, w_branch, tm):
    s = abc.shape[0]
    d = w_branch.shape[2]
    return pl.pallas_call(
        _merge_kernel,
        out_shape=jax.ShapeDtypeStruct((s, d), BF16),
        grid=(s // tm,),
        in_specs=[pl.BlockSpec((tm, 3 * BRANCH_W), lambda i: (i, 0)),
                  pl.BlockSpec((tm, BRANCH_W), lambda i: (i, 0)),
                  pl.BlockSpec((tm, N_BRANCH * d), lambda i: (i, 0)),
                  _resident((N_BRANCH, BRANCH_W, d), lambda i: (0, 0, 0))],
        out_specs=pl.BlockSpec((tm, d), lambda i: (i, 0)),
        compiler_params=_params("parallel"),
        name="merge",
    )(abc, od, gates, w_branch)


def _outproj_kernel(m_ref, w_ref, h_ref, g_ref, h1_ref, hn_ref):
    h1 = h_ref[...] + jnp.dot(m_ref[...], w_ref[...], preferred_element_type=F32)
    h1_ref[...] = h1
    hn_ref[...] = _rmsnorm_rows(h1, g_ref[...]).astype(hn_ref.dtype)


def _outproj(merged, w_out, h, g2, tm):
    s, d = h.shape
    return pl.pallas_call(
        _outproj_kernel,
        out_shape=(jax.ShapeDtypeStruct((s, d), F32), jax.ShapeDtypeStruct((s, d), BF16)),
        grid=(s // tm,),
        in_specs=[pl.BlockSpec((tm, d), lambda i: (i, 0)),
                  _resident((d, d), lambda i: (0, 0)),
                  pl.BlockSpec((tm, d), lambda i: (i, 0)),
                  pl.BlockSpec((1, d), lambda i: (0, 0))],
        out_specs=(pl.BlockSpec((tm, d), lambda i: (i, 0)),
                   pl.BlockSpec((tm, d), lambda i: (i, 0))),
        compiler_params=_params("parallel"),
        name="outproj",
    )(merged, w_out, h, g2)


N_CAND = 2 * PEER_TOPK + 7 * 8


def _top16(s, sub_iota):
    t = s.shape[1]
    row16 = lax.broadcasted_iota(jnp.int32, (PEER_TOPK, t), 0)
    rank = jnp.full(s.shape, float(PEER_TOPK), F32)
    vals = jnp.zeros((PEER_TOPK, t), F32)
    work = s
    for r in range(PEER_TOPK):
        m = jnp.max(work, axis=0, keepdims=True)
        first = jnp.min(jnp.where(work == m, sub_iota, float(PEER_KEYS)), axis=0, keepdims=True)
        hit = sub_iota == first
        rank = jnp.where(hit, float(r), rank)
        work = jnp.where(hit, -jnp.inf, work)
        vals = jnp.where(row16 == r, m, vals)
    return rank, vals


def _retrieve_kernel(hn_ref, wq_ref, keys_ref, r1_ref, e1_ref, q_ref, c_ref):
    t = hn_ref.shape[0]
    qf = jnp.dot(hn_ref[...], wq_ref[...], preferred_element_type=F32).astype(BF16)
    sub_iota = lax.broadcasted_iota(jnp.int32, (PEER_KEYS, t), 0).astype(F32)
    row8 = lax.broadcasted_iota(jnp.int32, (8, t), 0)
    row16 = lax.broadcasted_iota(jnp.int32, (PEER_TOPK, t), 0)
    pos = jnp.concatenate(
        [row16.astype(F32)]
        + [(row8 + 16 * p).astype(F32) for p in range(1, 8)]
        + [((row8 + 8) * 16).astype(F32)], axis=0)

    for h in range(PEER_HEADS):
        sc = []
        for half in range(2):
            col = (h * 2 + half) * PEER_HALF
            sc.append(lax.dot_general(keys_ref[h, half], qf[:, col:col + PEER_HALF],
                                      (((1,), (1,)), ((), ())), preferred_element_type=F32))
        rank0, a = _top16(sc[0], sub_iota)
        rank1, b = _top16(sc[1], sub_iota)
        ea = jnp.exp(a - a[0:1])
        eb = jnp.exp(b - b[0:1])

        def pair(x, y, op):
            return jnp.concatenate(
                [op(x[0:1], y)] + [op(x[p:p + 1], y[0:8]) for p in range(1, 8)] + [op(x[8:16], y[0:1])],
                axis=0)

        cand = pair(a, b, jnp.add)
        wgt = pair(ea, eb, jnp.multiply)
        sel = jnp.zeros(cand.shape, F32)
        work = cand
        for _ in range(PEER_TOPK):
            m = jnp.max(work, axis=0, keepdims=True)
            first = jnp.min(jnp.where(work == m, pos, 1e9), axis=0, keepdims=True)
            hit = pos == first
            sel = jnp.where(hit, 1.0, sel)
            work = jnp.where(hit, -jnp.inf, work)
        z = jnp.sum(sel * wgt, axis=0, keepdims=True)
        cnt_lo = jnp.zeros((8, t), F32)
        cnt_lo = jnp.where(row8 == 0, jnp.sum(sel[0:16], axis=0, keepdims=True), cnt_lo)
        for p in range(1, 8):
            cnt_lo = jnp.where(row8 == p, jnp.sum(sel[8 + 8 * p:16 + 8 * p], axis=0, keepdims=True), cnt_lo)
        cnt = jnp.concatenate([cnt_lo, sel[72:80]], axis=0)
        qmap = jnp.zeros((PEER_KEYS, t), F32)
        for r in range(PEER_TOPK):
            qmap = jnp.where(rank0 == float(r), cnt[r:r + 1], qmap)
        r1_ref[h] = rank1.astype(r1_ref.dtype)
        e1_ref[h] = jnp.exp(sc[1] - b[0:1]).astype(e1_ref.dtype)
        q_ref[h] = qmap
        c_ref[h] = jnp.exp(sc[0] - a[0:1]) / z


def _retrieve(hn, wq, keys, tt):
    s, d = hn.shape
    shape = jax.ShapeDtypeStruct((PEER_HEADS, PEER_KEYS, s), F32)
    shape_lo = jax.ShapeDtypeStruct((PEER_HEADS, PEER_KEYS, s), BF16)
    out_spec = pl.BlockSpec((PEER_HEADS, PEER_KEYS, tt), lambda i: (0, 0, i))
    return pl.pallas_call(
        _retrieve_kernel,
        out_shape=(shape_lo, shape_lo, shape, shape),
        grid=(s // tt,),
        in_specs=[pl.BlockSpec((tt, d), lambda i: (i, 0)),
                  _resident(wq.shape, lambda i: (0, 0)),
                  _resident(keys.shape, lambda i: (0, 0, 0, 0))],
        out_specs=(out_spec, out_spec, out_spec, out_spec),
        compiler_params=_params("parallel"),
        name="peer_retrieve",
    )(hn, wq, keys)


EXPERT_BLOCK = 8 * PEER_KEYS
EXPERT_CHAIN = 4 * PEER_KEYS


def _experts_kernel(hnt_ref, u_ref, vt_ref, r1_ref, e1_ref, q_ref, c_ref, o_ref):
    @pl.when(pl.program_id(1) == 0)
    def _():
        o_ref[...] = jnp.zeros(o_ref.shape, o_ref.dtype)

    tt = hnt_ref.shape[1]
    lo = e1_ref.dtype
    zero = jnp.zeros((), lo)
    n_i = EXPERT_CHAIN // PEER_KEYS
    for ch in range(EXPERT_BLOCK // EXPERT_CHAIN):
        rows = slice(ch * EXPERT_CHAIN, (ch + 1) * EXPERT_CHAIN)
        act = _gelu_tanh(jnp.dot(u_ref[rows], hnt_ref[...], preferred_element_type=F32))
        blocks = []
        for ii in range(n_i):
            i = ch * n_i + ii
            w = None
            for h in range(PEER_HEADS):
                qrow = q_ref[h, i:i + 1, :].astype(lo)
                crow = c_ref[h, i:i + 1, :].astype(lo)
                gate = jnp.where(r1_ref[h] < qrow, e1_ref[h] * crow, zero)
                w = gate if w is None else w + gate
            blocks.append(w.astype(F32) * act[ii * PEER_KEYS:(ii + 1) * PEER_KEYS])
        a_t = jnp.concatenate(blocks, axis=0).astype(vt_ref.dtype)
        o_ref[...] += jnp.dot(vt_ref[:, rows], a_t, preferred_element_type=F32)


def _experts(hn_t, u_tab, v_tab_t, r1, e1, q, c, tt):
    d, s = hn_t.shape
    n_exp = u_tab.shape[0]
    eb = EXPERT_BLOCK
    n_chain = eb // EXPERT_CHAIN
    return pl.pallas_call(
        _experts_kernel,
        out_shape=jax.ShapeDtypeStruct((d, s), F32),
        grid=(s // tt, n_exp // eb),
        in_specs=[pl.BlockSpec((d, tt), lambda i, e: (0, i), pipeline_mode=pl.Buffered(1)),
                  pl.BlockSpec((eb, d), lambda i, e: (e, 0)),
                  pl.BlockSpec((d, eb), lambda i, e: (0, e)),
                  pl.BlockSpec((PEER_HEADS, PEER_KEYS, tt), lambda i, e: (0, 0, i),
                               pipeline_mode=pl.Buffered(1)),
                  pl.BlockSpec((PEER_HEADS, PEER_KEYS, tt), lambda i, e: (0, 0, i),
                               pipeline_mode=pl.Buffered(1)),
                  pl.BlockSpec((PEER_HEADS, eb // PEER_KEYS, tt), lambda i, e: (0, e, i)),
                  pl.BlockSpec((PEER_HEADS, eb // PEER_KEYS, tt), lambda i, e: (0, e, i))],
        out_specs=pl.BlockSpec((d, tt), lambda i, e: (0, i)),
        compiler_params=_params("parallel", "arbitrary"),
        name="peer_experts",
    )(hn_t, u_tab, v_tab_t, r1, e1, q, c)


def _final_kernel(h_ref, dl_ref, g_ref, o_ref):
    o_ref[...] = _rmsnorm_rows(h_ref[...] + dl_ref[...], g_ref[...])


def _final_norm(h, delta, g, tm):
    s, d = h.shape
    row_spec = pl.BlockSpec((tm, d), lambda i: (i, 0))
    return pl.pallas_call(
        _final_kernel,
        out_shape=jax.ShapeDtypeStruct((s, d), F32),
        grid=(s // tm,),
        in_specs=[row_spec, row_spec, pl.BlockSpec((1, d), lambda i: (0, 0))],
        out_specs=row_spec,
        compiler_params=_params("parallel"),
        name="final_norm",
    )(h, delta, g)


def _tile(s, want):
    t = min(s, want)
    assert s % t == 0
    return t


def _layer(h, delta, p):
    s, d = h.shape
    zmix, fg, xn, h = _inproj(h, delta, p["norm1_g"], p["w_mix"], p["w_fg"], _tile(s, 512), 512)
    gates = _gates(xn, p["w_gate"], _tile(s, 1024), 1024)

    f_t = jnp.transpose(fg[:, :8])
    f_cum = _fcum(f_t, p["forget_b"])
    f_rows = f_cum[:GROUPS].reshape(GROUPS, 1, s)
    od = _attention(zmix, f_rows, _tile(s, 512), _tile(s, 512))

    abc = _mixers(zmix, p["conv_w"], p["sgu_norm_g"], p["sgu_w"], p["sgu_b"], p["pool_w"],
                  p["pool_scale"], _tile(s, 512))
    merged = _merge(abc, od, gates, p["w_branch"], _tile(s, 512))
    h1, hn = _outproj(merged, p["w_out"], h, p["norm2_g"], _tile(s, 512))

    r1, e1, q, c = _retrieve(hn, p["peer_wq"], p["peer_keys"], _tile(s, 512))
    pe_t = _experts(jnp.transpose(hn), p["peer_u"], p["peer_v_t"], r1, e1, q, c, _tile(s, 1024))
    return h1, jnp.transpose(pe_t)


def kernel(x, norm1_g, w_in, conv_w, sgu_norm_g, sgu_w, sgu_b, pool_w, pool_scale, forget_b,
           w_branch, w_out, norm2_g, peer_wq, peer_keys, peer_u, peer_v, final_g):
    bsz, s, d = x.shape
    depth = w_in.shape[0]
    off_g = N_MIX + GROUPS
    assert w_in.shape[2] == off_g + N_BRANCH * d
    tril = jnp.tril(jnp.ones((CHUNK, CHUNK), dtype=bool))

    outs = []
    for b in range(bsz):
        h, delta = x[b], None
        for l in range(depth):
            p = {
                "norm1_g": norm1_g[l][None, :],
                "w_mix": w_in[l, :, :N_MIX].astype(BF16),
                "w_fg": jnp.pad(w_in[l, :, N_MIX:off_g], ((0, 0), (0, FG_PAD - GROUPS))).astype(BF16),
                "w_gate": w_in[l, :, off_g:].astype(BF16),
                "conv_w": conv_w[l],
                "sgu_norm_g": sgu_norm_g[l][None, :],
                "sgu_w": jnp.where(tril[None], sgu_w[l], 0.0).astype(BF16),
                "sgu_b": jnp.broadcast_to(sgu_b[l][:, :, None], (GROUPS, CHUNK, HEAD_DIM)),
                "pool_w": pool_w[l].astype(BF16),
                "pool_scale": pool_scale[l][None, :],
                "forget_b": jnp.pad(forget_b[l], (0, 8 - GROUPS))[:, None],
                "w_branch": w_branch[l].astype(BF16),
                "w_out": w_out[l].astype(BF16),
                "norm2_g": norm2_g[l][None, :],
                "peer_wq": peer_wq[l].astype(BF16),
                "peer_keys": peer_keys[l].astype(BF16),
                "peer_u": peer_u[l].astype(BF16),
                "peer_v_t": jnp.transpose(peer_v[l].astype(BF16)),
            }
            h, delta = _layer(h, delta, p)
        outs.append(_final_norm(h, delta, final_g[None, :], _tile(s, 512)))
    return jnp.stack(outs, axis=0)
```

```python
import functools
import math

import numpy as np
import jax
import jax.numpy as jnp
from jax import lax
from jax.experimental import pallas as pl
from jax.experimental.pallas import tpu as pltpu

F32 = jnp.float32
BF16 = jnp.bfloat16

HEAD_DIM = 128
GROUPS = 4
BRANCH_W = GROUPS * HEAD_DIM
N_BRANCH = 4
CONV_W = 3
CHUNK = 128
POOL_WINDOWS = (2, 4, 8, 16)
PEER_HEADS = 8
PEER_KEYS = 128
PEER_HALF = 128
PEER_TOPK = 16
EPS = 1e-6

OFF_B = 3 * BRANCH_W
OFF_C = OFF_B + 2 * BRANCH_W
OFF_D = OFF_C + BRANCH_W
N_MIX = OFF_D + 3 * BRANCH_W
HIST = 16
FG_PAD = 128

V7X_VMEM_LIMIT_BYTES = 56 * 1024 * 1024
NEG_BIG = -1e30
LOG2E = math.log2(math.e)


def _params(*sem):
    return pltpu.CompilerParams(dimension_semantics=sem, vmem_limit_bytes=V7X_VMEM_LIMIT_BYTES)


def _gelu_tanh(x):
    return 0.5 * x * (1.0 + jnp.tanh(math.sqrt(2.0 / math.pi) * (x + 0.044715 * (x * x * x))))


def _rmsnorm_rows(x, g):
    return x * lax.rsqrt(jnp.mean(x * x, axis=-1, keepdims=True) + EPS) * g


def _resident(shape, index_map):
    return pl.BlockSpec(shape, index_map, pipeline_mode=pl.Buffered(1))


def _addnorm_kernel(*refs, has_delta, emit_stream):
    h_ref, refs = refs[0], refs[1:]
    h = h_ref[...]
    if has_delta:
        h = h + jnp.transpose(refs[0][...])
        refs = refs[1:]
    g_ref, refs = refs[0], refs[1:]
    if emit_stream:
        refs[0][...] = h
        refs = refs[1:]
    refs[0][...] = _rmsnorm_rows(h, g_ref[...]).astype(refs[0].dtype)


def _addnorm(h, delta_t, g, out_dtype, tm, want_stream):
    s, d = h.shape
    has_delta = delta_t is not None
    row_spec = pl.BlockSpec((tm, d), lambda i: (i, 0))
    in_specs = [row_spec] + ([pl.BlockSpec((d, tm), lambda i: (0, i))] if has_delta else []) + [
        pl.BlockSpec((1, d), lambda i: (0, 0))]
    out_shape = [jax.ShapeDtypeStruct((s, d), out_dtype)]
    out_specs = [row_spec]
    emit_stream = has_delta and want_stream
    if emit_stream:
        out_shape.insert(0, jax.ShapeDtypeStruct((s, d), F32))
        out_specs.insert(0, row_spec)
    outs = pl.pallas_call(
        functools.partial(_addnorm_kernel, has_delta=has_delta, emit_stream=emit_stream),
        out_shape=tuple(out_shape),
        grid=(s // tm,),
        in_specs=in_specs,
        out_specs=tuple(out_specs),
        compiler_params=_params("parallel"),
        name="addnorm",
    )(*((h, delta_t) if has_delta else (h,)), g)
    return outs if emit_stream else (h, outs[0])


def _inproj_kernel(xn_ref, w_ref, wfg_ref, z_ref, fg_ref):
    @pl.when(pl.program_id(1) == 0)
    def _():
        fg_ref[...] = jnp.dot(xn_ref[...], wfg_ref[...], preferred_element_type=F32)

    z_ref[...] = jnp.dot(xn_ref[...], w_ref[...], preferred_element_type=F32)


def _inproj(xn, w_mix, w_fg, tm, tn):
    s, d = xn.shape
    n = w_mix.shape[1]
    return pl.pallas_call(
        _inproj_kernel,
        out_shape=(jax.ShapeDtypeStruct((s, n), F32), jax.ShapeDtypeStruct((s, FG_PAD), F32)),
        grid=(s // tm, n // tn),
        in_specs=[pl.BlockSpec((tm, d), lambda i, j: (i, 0)),
                  pl.BlockSpec((d, tn), lambda i, j: (0, j)),
                  pl.BlockSpec((d, FG_PAD), lambda i, j: (0, 0))],
        out_specs=(pl.BlockSpec((tm, tn), lambda i, j: (i, j)),
                   pl.BlockSpec((tm, FG_PAD), lambda i, j: (i, 0))),
        compiler_params=_params("parallel", "arbitrary"),
        name="inproj",
    )(xn, w_mix, w_fg)


def _gates_kernel(xn_ref, w_ref, o_ref):
    z = jnp.dot(xn_ref[...], w_ref[...], preferred_element_type=F32)
    o_ref[...] = (1.0 / (1.0 + jnp.exp(-z))).astype(o_ref.dtype)


def _gates(xn, w_gate, tm, tn):
    s, d = xn.shape
    n = w_gate.shape[1]
    return pl.pallas_call(
        _gates_kernel,
        out_shape=jax.ShapeDtypeStruct((s, n), BF16),
        grid=(s // tm, n // tn),
        in_specs=[pl.BlockSpec((tm, d), lambda i, j: (i, 0)),
                  pl.BlockSpec((d, tn), lambda i, j: (0, j))],
        out_specs=pl.BlockSpec((tm, tn), lambda i, j: (i, j)),
        compiler_params=_params("parallel", "arbitrary"),
        name="gates",
    )(xn, w_gate)


CUM_BLOCK = 256


def _fcum_kernel(x_ref, b_ref, tri_ref, o_ref):
    s = x_ref.shape[1]
    carry = jnp.zeros((x_ref.shape[0], 1), F32)
    for c in range(s // CUM_BLOCK):
        sl = slice(c * CUM_BLOCK, (c + 1) * CUM_BLOCK)
        x = x_ref[:, sl] + b_ref[...]
        ls = jnp.minimum(x, 0.0) - jnp.log1p(jnp.exp(-jnp.abs(x)))
        y = jnp.dot(ls, tri_ref[...], precision=lax.Precision.HIGHEST,
                    preferred_element_type=F32) + carry
        o_ref[:, sl] = y
        carry = y[:, CUM_BLOCK - 1:CUM_BLOCK]


def _fcum(f_t, b_col):
    tri = jnp.asarray(np.triu(np.ones((CUM_BLOCK, CUM_BLOCK), np.float32)))
    return pl.pallas_call(
        _fcum_kernel,
        out_shape=jax.ShapeDtypeStruct(f_t.shape, F32),
        compiler_params=pltpu.CompilerParams(vmem_limit_bytes=V7X_VMEM_LIMIT_BYTES),
        name="fcum",
    )(f_t, b_col, tri)


def _attn_kernel(qi_ref, kj_ref, q_ref, k_ref, v_ref, fq_ref, fk_ref, o_ref, m_ref, l_ref, acc_ref,
                 *, tq, tk, scale):
    p = pl.program_id(0)
    qi = qi_ref[p]
    kj = kj_ref[p]

    @pl.when(kj == 0)
    def _():
        m_ref[...] = jnp.full(m_ref.shape, NEG_BIG, F32)
        l_ref[...] = jnp.zeros(l_ref.shape, F32)
        acc_ref[...] = jnp.zeros(acc_ref.shape, F32)

    def step(masked):
        if masked:
            rows = qi * tq + lax.broadcasted_iota(jnp.int32, (tq, tk), 0)
            cols = kj * tk + lax.broadcasted_iota(jnp.int32, (tq, tk), 1)
            keep = rows >= cols
        ones = jnp.ones((tk, HEAD_DIM), BF16)
        for h in range(GROUPS):
            cs = slice(h * HEAD_DIM, (h + 1) * HEAD_DIM)
            q = q_ref[:, cs].astype(BF16)
            k = k_ref[:, cs].astype(BF16)
            s = lax.dot_general(q, k, (((1,), (1,)), ((), ())), preferred_element_type=F32) * (scale * LOG2E)
            fq = fq_ref[h]
            s = s + (fq[:, 0:1] - fk_ref[h]) * LOG2E
            if masked:
                s = jnp.where(keep, s, NEG_BIG)
            m_prev = m_ref[h]
            m_new = jnp.maximum(m_prev, jnp.max(s, axis=-1, keepdims=True))
            alpha = jnp.exp2(m_prev - m_new)
            pr = jnp.exp2(s - jnp.concatenate([m_new] * (tk // HEAD_DIM), axis=1)).astype(BF16)
            pv = jnp.dot(pr, jnp.concatenate([v_ref[:, cs].astype(BF16), ones], axis=1),
                         preferred_element_type=F32)
            l_ref[h] = alpha * l_ref[h] + pv[:, HEAD_DIM:]
            acc_ref[:, cs] = alpha * acc_ref[:, cs] + pv[:, :HEAD_DIM]
            m_ref[h] = m_new

    last = (qi * tq + tq - 1) // tk
    first_masked = (qi * tq) // tk

    @pl.when(kj < first_masked)
    def _():
        step(False)

    @pl.when(kj >= first_masked)
    def _():
        step(True)

    @pl.when(kj == last)
    def _():
        for h in range(GROUPS):
            cs = slice(h * HEAD_DIM, (h + 1) * HEAD_DIM)
            o_ref[:, cs] = (acc_ref[:, cs] / l_ref[h]).astype(o_ref.dtype)


def _attention(zmix, f_rows, tq, tk):
    s = zmix.shape[0]
    nq = s // tq
    pairs = [(i, j) for i in range(nq) for j in range((i * tq + tq - 1) // tk + 1)]
    qi = jnp.asarray(np.array([p[0] for p in pairs], np.int32))
    kj = jnp.asarray(np.array([p[1] for p in pairs], np.int32))
    qcol = OFF_D // BRANCH_W
    grid_spec = pltpu.PrefetchScalarGridSpec(
        num_scalar_prefetch=2,
        grid=(len(pairs),),
        in_specs=[
            pl.BlockSpec((tq, BRANCH_W), lambda p, qi, kj: (qi[p], qcol)),
            pl.BlockSpec((tk, BRANCH_W), lambda p, qi, kj: (kj[p], qcol + 1)),
            pl.BlockSpec((tk, BRANCH_W), lambda p, qi, kj: (kj[p], qcol + 2)),
            pl.BlockSpec((GROUPS, 1, tq), lambda p, qi, kj: (0, 0, qi[p])),
            pl.BlockSpec((GROUPS, 1, tk), lambda p, qi, kj: (0, 0, kj[p])),
        ],
        out_specs=pl.BlockSpec((tq, BRANCH_W), lambda p, qi, kj: (qi[p], 0)),
        scratch_shapes=[pltpu.VMEM((GROUPS, tq, HEAD_DIM), F32), pltpu.VMEM((GROUPS, tq, HEAD_DIM), F32),
                        pltpu.VMEM((tq, BRANCH_W), F32)],
    )
    return pl.pallas_call(
        functools.partial(_attn_kernel, tq=tq, tk=tk, scale=HEAD_DIM ** -0.5),
        out_shape=jax.ShapeDtypeStruct((s, BRANCH_W), BF16),
        grid_spec=grid_spec,
        compiler_params=_params("arbitrary"),
        name="fox_attention",
    )(qi, kj, zmix, zmix, zmix, f_rows, f_rows)


def _mix_kernel(z_ref, zp_ref, cw_ref, ng_ref, ws_ref, sb_ref, pw_ref, ps_ref, o_ref, ext_ref, *, tb):
    i = pl.program_id(0)
    bw = BRANCH_W

    prev = zp_ref[...]
    prev = jnp.where(i > 0, prev, jnp.zeros_like(prev))
    ext_ref[0:HIST, 0:bw] = prev[:, bw:2 * bw] * prev[:, 2 * bw:3 * bw]
    ext_ref[HIST:HIST + tb, 0:bw] = z_ref[:, bw:2 * bw] * z_ref[:, 2 * bw:3 * bw]
    ext_ref[0:HIST, bw:2 * bw] = prev[:, OFF_C:OFF_C + bw]
    ext_ref[HIST:HIST + tb, bw:2 * bw] = z_ref[:, OFF_C:OFF_C + bw]
    conv = cw_ref[CONV_W - 1:CONV_W, :] * ext_ref[HIST:HIST + tb, 0:bw]
    for lag in range(1, CONV_W):
        conv = conv + cw_ref[CONV_W - 1 - lag:CONV_W - lag, :] * ext_ref[HIST - lag:HIST - lag + tb, 0:bw]
    o_ref[:, 0:bw] = (z_ref[:, 0:bw] * conv).astype(o_ref.dtype)

    u = _gelu_tanh(z_ref[:, OFF_B:OFF_B + bw])
    v = _gelu_tanh(z_ref[:, OFF_B + bw:OFF_B + 2 * bw])
    v = _rmsnorm_rows(v, ng_ref[...]).astype(BF16)
    for c in range(tb // CHUNK):
        rs = slice(c * CHUNK, (c + 1) * CHUNK)
        for g in range(GROUPS):
            cs = slice(g * HEAD_DIM, (g + 1) * HEAD_DIM)
            sv = jnp.dot(ws_ref[g], v[rs, cs], preferred_element_type=F32) + sb_ref[g]
            o_ref[rs, bw + g * HEAD_DIM:bw + (g + 1) * HEAD_DIM] = (u[rs, cs] * sv).astype(o_ref.dtype)

    t = (i * tb + 1 + lax.broadcasted_iota(jnp.int32, (tb, 1), 0)).astype(F32)
    for g, w in enumerate(POOL_WINDOWS):
        cs = slice(bw + g * HEAD_DIM, bw + (g + 1) * HEAD_DIM)
        cur = ext_ref[HIST:HIST + tb, cs]
        tot = cur
        for lag in range(1, w):
            tot = tot + ext_ref[HIST - lag:HIST - lag + tb, cs]
        pooled = (tot / jnp.minimum(t, float(w)) - cur).astype(BF16)
        y = jnp.dot(pooled, pw_ref[g], preferred_element_type=F32)
        y = y * ps_ref[:, g * HEAD_DIM:(g + 1) * HEAD_DIM]
        o_ref[:, 2 * bw + g * HEAD_DIM:2 * bw + (g + 1) * HEAD_DIM] = y.astype(o_ref.dtype)


def _mixers(zmix, conv_w, sgu_norm_g, sgu_w_tril, sgu_bias_b, pool_w, pool_scale, tb):
    s = zmix.shape[0]
    ncol = OFF_D
    hist_blocks = tb // HIST
    const2 = lambda i: (0, 0)
    const3 = lambda i: (0, 0, 0)
    return pl.pallas_call(
        functools.partial(_mix_kernel, tb=tb),
        out_shape=jax.ShapeDtypeStruct((s, 3 * BRANCH_W), BF16),
        grid=(s // tb,),
        in_specs=[pl.BlockSpec((tb, ncol), lambda i: (i, 0)),
                  pl.BlockSpec((HIST, ncol), lambda i: (jnp.maximum(i * hist_blocks - 1, 0), 0)),
                  pl.BlockSpec((CONV_W, BRANCH_W), const2),
                  pl.BlockSpec((1, BRANCH_W), const2),
                  pl.BlockSpec((GROUPS, CHUNK, CHUNK), const3),
                  pl.BlockSpec((GROUPS, CHUNK, HEAD_DIM), const3),
                  pl.BlockSpec((GROUPS, HEAD_DIM, HEAD_DIM), const3),
                  pl.BlockSpec((1, BRANCH_W), const2)],
        out_specs=pl.BlockSpec((tb, 3 * BRANCH_W), lambda i: (i, 0)),
        scratch_shapes=[pltpu.VMEM((HIST + tb, 2 * BRANCH_W), F32)],
        compiler_params=_params("parallel"),
        name="mixers",
    )(zmix, zmix, conv_w, sgu_norm_g, sgu_w_tril, sgu_bias_b, pool_w, pool_scale)


def _merge_kernel(abc_ref, od_ref, gates_ref, wb_ref, o_ref):
    d = o_ref.shape[1]
    acc = None
    for n in range(N_BRANCH):
        br = od_ref[...] if n == N_BRANCH - 1 else abc_ref[:, n * BRANCH_W:(n + 1) * BRANCH_W]
        y = jnp.dot(br, wb_ref[n], preferred_element_type=F32)
        term = gates_ref[:, n * d:(n + 1) * d].astype(F32) * y
        acc = term if acc is None else acc + term
    o_ref[...] = acc.astype(o_ref.dtype)


def _merge(abc, od, gates, w_branch, tm):
    s = abc.shape[0]
    d = w_branch.shape[2]
    return pl.pallas_call(
        _merge_kernel,
        out_shape=jax.ShapeDtypeStruct((s, d), BF16),
        grid=(s // tm,),
        in_specs=[pl.BlockSpec((tm, 3 * BRANCH_W), lambda i: (i, 0)),
                  pl.BlockSpec((tm, BRANCH_W), lambda i: (i, 0)),
                  pl.BlockSpec((tm, N_BRANCH * d), lambda i: (i, 0)),
                  _resident((N_BRANCH, BRANCH_W, d), lambda i: (0, 0, 0))],
        out_specs=pl.BlockSpec((tm, d), lambda i: (i, 0)),
        compiler_params=_params("parallel"),
        name="merge",
    )(abc, od, gates, w_branch)


def _outproj_kernel(m_ref, w_ref, h_ref, g_ref, h1_ref, hn_ref, hnt_ref):
    h1 = h_ref[...] + jnp.dot(m_ref[...], w_ref[...], preferred_element_type=F32)
    h1_ref[...] = h1
    hn = _rmsnorm_rows(h1, g_ref[...])
    hn_ref[...] = hn.astype(hn_ref.dtype)
    hnt_ref[...] = jnp.transpose(hn).astype(hnt_ref.dtype)


def _outproj(merged, w_out, h, g2, tm):
    s, d = h.shape
    return pl.pallas_call(
        _outproj_kernel,
        out_shape=(jax.ShapeDtypeStruct((s, d), F32), jax.ShapeDtypeStruct((s, d), BF16),
                   jax.ShapeDtypeStruct((d, s), BF16)),
        grid=(s // tm,),
        in_specs=[pl.BlockSpec((tm, d), lambda i: (i, 0)),
                  _resident((d, d), lambda i: (0, 0)),
                  pl.BlockSpec((tm, d), lambda i: (i, 0)),
                  pl.BlockSpec((1, d), lambda i: (0, 0))],
        out_specs=(pl.BlockSpec((tm, d), lambda i: (i, 0)),
                   pl.BlockSpec((tm, d), lambda i: (i, 0)),
                   pl.BlockSpec((d, tm), lambda i: (0, i))),
        compiler_params=_params("parallel"),
        name="outproj",
    )(merged, w_out, h, g2)


def _candidate_positions(t):
    row8 = lax.broadcasted_iota(jnp.int32, (8, t), 0)
    row16 = lax.broadcasted_iota(jnp.int32, (PEER_TOPK, t), 0)
    return jnp.concatenate(
        [row16.astype(F32)]
        + [(row8 + 16 * p).astype(F32) for p in range(1, 8)]
        + [((row8 + 8) * 16).astype(F32)], axis=0)


def _pair(x, y, op):
    return jnp.concatenate(
        [op(x[0:1], y)] + [op(x[p:p + 1], y[0:8]) for p in range(1, 8)] + [op(x[8:16], y[0:1])], axis=0)


def _top16(s, exact):
    t = s.shape[1]
    row16 = lax.broadcasted_iota(jnp.int32, (PEER_TOPK, t), 0)
    sub_iota = lax.broadcasted_iota(jnp.int32, s.shape, 0).astype(F32) if exact else None
    rank = jnp.full(s.shape, float(PEER_TOPK), F32)
    vals = jnp.zeros((PEER_TOPK, t), F32)
    work = s
    for r in range(PEER_TOPK):
        m = jnp.max(work, axis=0, keepdims=True)
        hit = work == m
        if exact:
            first = jnp.min(jnp.where(hit, sub_iota, float(PEER_KEYS)), axis=0, keepdims=True)
            hit = sub_iota == first
        rank = jnp.where(hit, float(r), rank)
        work = jnp.where(hit, -jnp.inf, work)
        vals = jnp.where(row16 == r, m, vals)
    count = jnp.sum(jnp.where(rank < float(PEER_TOPK), 1.0, 0.0), axis=0, keepdims=True)
    return rank, vals, count


def _select_pairs(cand, exact):
    pos = _candidate_positions(cand.shape[1]) if exact else None
    sel = jnp.zeros(cand.shape, F32)
    work = cand
    for _ in range(PEER_TOPK):
        m = jnp.max(work, axis=0, keepdims=True)
        hit = work == m
        if exact:
            first = jnp.min(jnp.where(hit, pos, 1e9), axis=0, keepdims=True)
            hit = pos == first
        sel = jnp.where(hit, 1.0, sel)
        work = jnp.where(hit, -jnp.inf, work)
    return sel, jnp.sum(sel, axis=0, keepdims=True)


def _retrieve_head(sc0, sc1, exact):
    t = sc0.shape[1]
    rank0, a, n0 = _top16(sc0, exact)
    rank1, b, n1 = _top16(sc1, exact)
    sel, n2 = _select_pairs(_pair(a, b, jnp.add), exact)
    wgt = _pair(jnp.exp(a - a[0:1]), jnp.exp(b - b[0:1]), jnp.multiply)
    z = jnp.sum(sel * wgt, axis=0, keepdims=True)
    row8 = lax.broadcasted_iota(jnp.int32, (8, t), 0)
    cnt_lo = jnp.zeros((8, t), F32)
    cnt_lo = jnp.where(row8 == 0, jnp.sum(sel[0:16], axis=0, keepdims=True), cnt_lo)
    for p in range(1, 8):
        cnt_lo = jnp.where(row8 == p, jnp.sum(sel[8 + 8 * p:16 + 8 * p], axis=0, keepdims=True), cnt_lo)
    cnt = jnp.concatenate([cnt_lo, sel[72:80]], axis=0)
    count = jnp.zeros((PEER_KEYS, t), F32)
    for r in range(PEER_TOPK):
        count = jnp.where(rank0 == float(r), cnt[r:r + 1], count)
    k = float(PEER_TOPK)
    ok = jnp.where((n0 == k) & (n1 == k) & (n2 == k), 1.0, 0.0)
    return rank1, jnp.exp(sc1 - b[0:1]), count, jnp.exp(sc0 - a[0:1]) / z, ok


def _retrieve_kernel(hn_ref, wq_ref, keys_ref, r1_ref, e1_ref, q_ref, c_ref, qs_ref):
    qf = jnp.dot(hn_ref[...], wq_ref[...], preferred_element_type=F32).astype(qs_ref.dtype)
    for n in range(2 * PEER_HEADS):
        qs_ref[n] = qf[:, n * PEER_HALF:(n + 1) * PEER_HALF]

    def head(h, carry):
        def scores(half):
            return lax.dot_general(keys_ref[h, half], qs_ref[2 * h + half],
                                   (((1,), (1,)), ((), ())), preferred_element_type=F32)

        def emit(exact):
            rank1, gate1, count, gate0, ok = _retrieve_head(scores(0), scores(1), exact)
            r1_ref[h] = rank1.astype(r1_ref.dtype)
            e1_ref[h] = gate1.astype(e1_ref.dtype)
            q_ref[h] = count
            c_ref[h] = gate0
            return ok

        ok = emit(False)

        @pl.when(jnp.min(ok) < 0.5)
        def _():
            emit(True)

        return carry

    lax.fori_loop(0, PEER_HEADS, head, 0)


def _retrieve(hn, wq, keys, tt):
    s, d = hn.shape
    shape = jax.ShapeDtypeStruct((PEER_HEADS, PEER_KEYS, s), F32)
    shape_lo = jax.ShapeDtypeStruct((PEER_HEADS, PEER_KEYS, s), BF16)
    out_spec = pl.BlockSpec((PEER_HEADS, PEER_KEYS, tt), lambda i: (0, 0, i))
    return pl.pallas_call(
        _retrieve_kernel,
        out_shape=(shape_lo, shape_lo, shape, shape),
        grid=(s // tt,),
        in_specs=[pl.BlockSpec((tt, d), lambda i: (i, 0)),
                  _resident(wq.shape, lambda i: (0, 0)),
                  _resident(keys.shape, lambda i: (0, 0, 0, 0))],
        out_specs=(out_spec, out_spec, out_spec, out_spec),
        scratch_shapes=[pltpu.VMEM((2 * PEER_HEADS, tt, PEER_HALF), BF16)],
        compiler_params=_params("parallel"),
        name="peer_retrieve",
    )(hn, wq, keys)


EXPERT_BLOCK = 8 * PEER_KEYS
EXPERT_CHAIN = 4 * PEER_KEYS


def _experts_kernel(hnt_ref, u_ref, vt_ref, r1_ref, e1_ref, q_ref, c_ref, o_ref):
    @pl.when(pl.program_id(1) == 0)
    def _():
        o_ref[...] = jnp.zeros(o_ref.shape, o_ref.dtype)

    tt = hnt_ref.shape[1]
    lo = e1_ref.dtype
    zero = jnp.zeros((), lo)
    n_i = EXPERT_CHAIN // PEER_KEYS
    for ch in range(EXPERT_BLOCK // EXPERT_CHAIN):
        rows = slice(ch * EXPERT_CHAIN, (ch + 1) * EXPERT_CHAIN)
        act = _gelu_tanh(jnp.dot(u_ref[rows], hnt_ref[...], preferred_element_type=F32))
        blocks = []
        for ii in range(n_i):
            i = ch * n_i + ii
            w = None
            for h in range(PEER_HEADS):
                qrow = q_ref[h, i:i + 1, :].astype(lo)
                crow = c_ref[h, i:i + 1, :].astype(lo)
                gate = jnp.where(r1_ref[h] < qrow, e1_ref[h] * crow, zero)
                w = gate if w is None else w + gate
            blocks.append(w.astype(F32) * act[ii * PEER_KEYS:(ii + 1) * PEER_KEYS])
        a_t = jnp.concatenate(blocks, axis=0).astype(vt_ref.dtype)
        o_ref[...] += jnp.dot(vt_ref[:, rows], a_t, preferred_element_type=F32)


def _experts(hn_t, u_tab, v_tab_t, r1, e1, q, c, tt):
    d, s = hn_t.shape
    n_exp = u_tab.shape[0]
    eb = EXPERT_BLOCK
    n_chain = eb // EXPERT_CHAIN
    return pl.pallas_call(
        _experts_kernel,
        out_shape=jax.ShapeDtypeStruct((d, s), F32),
        grid=(s // tt, n_exp // eb),
        in_specs=[pl.BlockSpec((d, tt), lambda i, e: (0, i), pipeline_mode=pl.Buffered(1)),
                  pl.BlockSpec((eb, d), lambda i, e: (e, 0)),
                  pl.BlockSpec((d, eb), lambda i, e: (0, e)),
                  pl.BlockSpec((PEER_HEADS, PEER_KEYS, tt), lambda i, e: (0, 0, i),
                               pipeline_mode=pl.Buffered(1)),
                  pl.BlockSpec((PEER_HEADS, PEER_KEYS, tt), lambda i, e: (0, 0, i),
                               pipeline_mode=pl.Buffered(1)),
                  pl.BlockSpec((PEER_HEADS, eb // PEER_KEYS, tt), lambda i, e: (0, e, i)),
                  pl.BlockSpec((PEER_HEADS, eb // PEER_KEYS, tt), lambda i, e: (0, e, i))],
        out_specs=pl.BlockSpec((d, tt), lambda i, e: (0, i)),
        compiler_params=_params("parallel", "arbitrary"),
        name="peer_experts",
    )(hn_t, u_tab, v_tab_t, r1, e1, q, c)


def _tile(s, want):
    t = min(s, want)
    assert s % t == 0
    return t


def _layer(h, delta_t, p):
    s, d = h.shape
    h, xn = _addnorm(h, delta_t, p["norm1_g"], BF16, _tile(s, 512), True)
    zmix, fg = _inproj(xn, p["w_mix"], p["w_fg"], _tile(s, 1024), 1536)
    gates = _gates(xn, p["w_gate"], _tile(s, 1024), 1024)

    f_t = jnp.transpose(fg[:, :8])
    f_cum = _fcum(f_t, p["forget_b"])
    f_rows = f_cum[:GROUPS].reshape(GROUPS, 1, s)
    od = _attention(zmix, f_rows, _tile(s, 512), _tile(s, 512))

    abc = _mixers(zmix, p["conv_w"], p["sgu_norm_g"], p["sgu_w"], p["sgu_b"], p["pool_w"],
                  p["pool_scale"], _tile(s, 512))
    merged = _merge(abc, od, gates, p["w_branch"], _tile(s, 512))
    h1, hn, hn_t = _outproj(merged, p["w_out"], h, p["norm2_g"], _tile(s, 512))

    r1, e1, q, c = _retrieve(hn, p["peer_wq"], p["peer_keys"], _tile(s, 512))
    return h1, _experts(hn_t, p["peer_u"], p["peer_v_t"], r1, e1, q, c, _tile(s, 1024))


def kernel(x, norm1_g, w_in, conv_w, sgu_norm_g, sgu_w, sgu_b, pool_w, pool_scale, forget_b,
           w_branch, w_out, norm2_g, peer_wq, peer_keys, peer_u, peer_v, final_g):
    bsz, s, d = x.shape
    depth = w_in.shape[0]
    off_g = N_MIX + GROUPS
    assert w_in.shape[2] == off_g + N_BRANCH * d
    tril = jnp.tril(jnp.ones((CHUNK, CHUNK), dtype=bool))

    outs = []
    for b in range(bsz):
        h, delta = x[b], None
        for l in range(depth):
            p = {
                "norm1_g": norm1_g[l][None, :],
                "w_mix": w_in[l, :, :N_MIX].astype(BF16),
                "w_fg": jnp.pad(w_in[l, :, N_MIX:off_g], ((0, 0), (0, FG_PAD - GROUPS))).astype(BF16),
                "w_gate": w_in[l, :, off_g:].astype(BF16),
                "conv_w": conv_w[l],
                "sgu_norm_g": sgu_norm_g[l][None, :],
                "sgu_w": jnp.where(tril[None], sgu_w[l], 0.0).astype(BF16),
                "sgu_b": jnp.broadcast_to(sgu_b[l][:, :, None], (GROUPS, CHUNK, HEAD_DIM)),
                "pool_w": pool_w[l].astype(BF16),
                "pool_scale": pool_scale[l][None, :],
                "forget_b": jnp.pad(forget_b[l], (0, 8 - GROUPS))[:, None],
                "w_branch": w_branch[l].astype(BF16),
                "w_out": w_out[l].astype(BF16),
                "norm2_g": norm2_g[l][None, :],
                "peer_wq": peer_wq[l].astype(BF16),
                "peer_keys": peer_keys[l].astype(BF16),
                "peer_u": peer_u[l].astype(BF16),
                "peer_v_t": jnp.transpose(peer_v[l].astype(BF16)),
            }
            h, delta = _layer(h, delta, p)
        outs.append(_addnorm(h, delta, final_g[None, :], F32, _tile(s, 512), False)[1])
    return jnp.stack(outs, axis=0)
```

```python
import functools
import math

import numpy as np
import jax
import jax.numpy as jnp
from jax import lax
from jax.experimental import pallas as pl
from jax.experimental.pallas import tpu as pltpu

F32 = jnp.float32
BF16 = jnp.bfloat16

HEAD_DIM = 128
GROUPS = 4
BRANCH_W = GROUPS * HEAD_DIM
N_BRANCH = 4
CONV_W = 3
CHUNK = 128
POOL_WINDOWS = (2, 4, 8, 16)
PEER_HEADS = 8
PEER_KEYS = 128
PEER_HALF = 128
PEER_TOPK = 16
EPS = 1e-6

OFF_B = 3 * BRANCH_W
OFF_C = OFF_B + 2 * BRANCH_W
OFF_D = OFF_C + BRANCH_W
N_MIX = OFF_D + 3 * BRANCH_W
HIST = 16
FG_PAD = 128

V7X_VMEM_LIMIT_BYTES = 56 * 1024 * 1024
NEG_BIG = -1e30
LOG2E = math.log2(math.e)


def _params(*sem):
    return pltpu.CompilerParams(dimension_semantics=sem, vmem_limit_bytes=V7X_VMEM_LIMIT_BYTES)


def _gelu_tanh(x):
    return 0.5 * x * (1.0 + jnp.tanh(math.sqrt(2.0 / math.pi) * (x + 0.044715 * (x * x * x))))


def _rmsnorm_rows(x, g):
    return x * lax.rsqrt(jnp.mean(x * x, axis=-1, keepdims=True) + EPS) * g


def _resident(shape, index_map):
    return pl.BlockSpec(shape, index_map, pipeline_mode=pl.Buffered(1))


def _addnorm_kernel(*refs, has_delta, emit_stream):
    h_ref, refs = refs[0], refs[1:]
    h = h_ref[...]
    if has_delta:
        h = h + jnp.transpose(refs[0][...])
        refs = refs[1:]
    g_ref, refs = refs[0], refs[1:]
    if emit_stream:
        refs[0][...] = h
        refs = refs[1:]
    refs[0][...] = _rmsnorm_rows(h, g_ref[...]).astype(refs[0].dtype)


def _addnorm(h, delta_t, g, out_dtype, tm, want_stream):
    s, d = h.shape
    has_delta = delta_t is not None
    row_spec = pl.BlockSpec((tm, d), lambda i: (i, 0))
    in_specs = [row_spec] + ([pl.BlockSpec((d, tm), lambda i: (0, i))] if has_delta else []) + [
        pl.BlockSpec((1, d), lambda i: (0, 0))]
    out_shape = [jax.ShapeDtypeStruct((s, d), out_dtype)]
    out_specs = [row_spec]
    emit_stream = has_delta and want_stream
    if emit_stream:
        out_shape.insert(0, jax.ShapeDtypeStruct((s, d), F32))
        out_specs.insert(0, row_spec)
    outs = pl.pallas_call(
        functools.partial(_addnorm_kernel, has_delta=has_delta, emit_stream=emit_stream),
        out_shape=tuple(out_shape),
        grid=(s // tm,),
        in_specs=in_specs,
        out_specs=tuple(out_specs),
        compiler_params=_params("parallel"),
        name="addnorm",
    )(*((h, delta_t) if has_delta else (h,)), g)
    return outs if emit_stream else (h, outs[0])


def _inproj_kernel(xn_ref, w_ref, wfg_ref, z_ref, fg_ref):
    @pl.when(pl.program_id(1) == 0)
    def _():
        fg_ref[...] = jnp.dot(xn_ref[...], wfg_ref[...], preferred_element_type=F32)

    z_ref[...] = jnp.dot(xn_ref[...], w_ref[...], preferred_element_type=F32)


def _inproj(xn, w_mix, w_fg, l, tm, tn):
    s, d = xn.shape
    n = w_mix.shape[2]
    return pl.pallas_call(
        _inproj_kernel,
        out_shape=(jax.ShapeDtypeStruct((s, n), F32), jax.ShapeDtypeStruct((s, FG_PAD), F32)),
        grid=(s // tm, n // tn),
        in_specs=[pl.BlockSpec((tm, d), lambda i, j: (i, 0)),
                  pl.BlockSpec((None, d, tn), lambda i, j: (l, 0, j)),
                  pl.BlockSpec((None, d, FG_PAD), lambda i, j: (l, 0, 0))],
        out_specs=(pl.BlockSpec((tm, tn), lambda i, j: (i, j)),
                   pl.BlockSpec((tm, FG_PAD), lambda i, j: (i, 0))),
        compiler_params=_params("parallel", "arbitrary"),
        name="inproj",
    )(xn, w_mix, w_fg)


def _gates_kernel(xn_ref, w_ref, o_ref):
    z = jnp.dot(xn_ref[...], w_ref[...], preferred_element_type=F32)
    o_ref[...] = (1.0 / (1.0 + jnp.exp(-z))).astype(o_ref.dtype)


def _gates(xn, w_gate, l, tm, tn):
    s, d = xn.shape
    n = w_gate.shape[2]
    return pl.pallas_call(
        _gates_kernel,
        out_shape=jax.ShapeDtypeStruct((s, n), BF16),
        grid=(s // tm, n // tn),
        in_specs=[pl.BlockSpec((tm, d), lambda i, j: (i, 0)),
                  pl.BlockSpec((None, d, tn), lambda i, j: (l, 0, j))],
        out_specs=pl.BlockSpec((tm, tn), lambda i, j: (i, j)),
        compiler_params=_params("parallel", "arbitrary"),
        name="gates",
    )(xn, w_gate)


CUM_BLOCK = 256


def _fcum_kernel(x_ref, b_ref, tri_ref, o_ref):
    s = x_ref.shape[1]
    carry = jnp.zeros((x_ref.shape[0], 1), F32)
    for c in range(s // CUM_BLOCK):
        sl = slice(c * CUM_BLOCK, (c + 1) * CUM_BLOCK)
        x = x_ref[:, sl] + b_ref[...]
        ls = jnp.minimum(x, 0.0) - jnp.log1p(jnp.exp(-jnp.abs(x)))
        y = jnp.dot(ls, tri_ref[...], precision=lax.Precision.HIGHEST,
                    preferred_element_type=F32) + carry
        o_ref[:, sl] = y
        carry = y[:, CUM_BLOCK - 1:CUM_BLOCK]


def _fcum(f_t, b_col):
    tri = jnp.asarray(np.triu(np.ones((CUM_BLOCK, CUM_BLOCK), np.float32)))
    return pl.pallas_call(
        _fcum_kernel,
        out_shape=jax.ShapeDtypeStruct(f_t.shape, F32),
        compiler_params=pltpu.CompilerParams(vmem_limit_bytes=V7X_VMEM_LIMIT_BYTES),
        name="fcum",
    )(f_t, b_col, tri)


def _attn_kernel(qi_ref, kj_ref, q_ref, k_ref, v_ref, fq_ref, fk_ref, o_ref, m_ref, l_ref, acc_ref,
                 *, tq, tk, scale):
    p = pl.program_id(0)
    qi = qi_ref[p]
    kj = kj_ref[p]

    @pl.when(kj == 0)
    def _():
        m_ref[...] = jnp.full(m_ref.shape, NEG_BIG, F32)
        l_ref[...] = jnp.zeros(l_ref.shape, F32)
        acc_ref[...] = jnp.zeros(acc_ref.shape, F32)

    def step(masked):
        if masked:
            rows = qi * tq + lax.broadcasted_iota(jnp.int32, (tq, tk), 0)
            cols = kj * tk + lax.broadcasted_iota(jnp.int32, (tq, tk), 1)
            keep = rows >= cols
        ones = jnp.ones((tk, HEAD_DIM), BF16)
        for h in range(GROUPS):
            cs = slice(h * HEAD_DIM, (h + 1) * HEAD_DIM)
            q = q_ref[:, cs].astype(BF16)
            k = k_ref[:, cs].astype(BF16)
            s = lax.dot_general(q, k, (((1,), (1,)), ((), ())), preferred_element_type=F32) * (scale * LOG2E)
            fq = fq_ref[h]
            s = s + (fq[:, 0:1] - fk_ref[h]) * LOG2E
            if masked:
                s = jnp.where(keep, s, NEG_BIG)
            m_prev = m_ref[h]
            m_new = jnp.maximum(m_prev, jnp.max(s, axis=-1, keepdims=True))
            alpha = jnp.exp2(m_prev - m_new)
            pr = jnp.exp2(s - jnp.concatenate([m_new] * (tk // HEAD_DIM), axis=1)).astype(BF16)
            pv = jnp.dot(pr, jnp.concatenate([v_ref[:, cs].astype(BF16), ones], axis=1),
                         preferred_element_type=F32)
            l_ref[h] = alpha * l_ref[h] + pv[:, HEAD_DIM:]
            acc_ref[:, cs] = alpha * acc_ref[:, cs] + pv[:, :HEAD_DIM]
            m_ref[h] = m_new

    last = (qi * tq + tq - 1) // tk
    first_masked = (qi * tq) // tk

    @pl.when(kj < first_masked)
    def _():
        step(False)

    @pl.when(kj >= first_masked)
    def _():
        step(True)

    @pl.when(kj == last)
    def _():
        for h in range(GROUPS):
            cs = slice(h * HEAD_DIM, (h + 1) * HEAD_DIM)
            o_ref[:, cs] = (acc_ref[:, cs] / l_ref[h]).astype(o_ref.dtype)


def _attention(zmix, f_rows, tq, tk):
    s = zmix.shape[0]
    nq = s // tq
    pairs = [(i, j) for i in range(nq) for j in range((i * tq + tq - 1) // tk + 1)]
    qi = jnp.asarray(np.array([p[0] for p in pairs], np.int32))
    kj = jnp.asarray(np.array([p[1] for p in pairs], np.int32))
    qcol = OFF_D // BRANCH_W
    grid_spec = pltpu.PrefetchScalarGridSpec(
        num_scalar_prefetch=2,
        grid=(len(pairs),),
        in_specs=[
            pl.BlockSpec((tq, BRANCH_W), lambda p, qi, kj: (qi[p], qcol)),
            pl.BlockSpec((tk, BRANCH_W), lambda p, qi, kj: (kj[p], qcol + 1)),
            pl.BlockSpec((tk, BRANCH_W), lambda p, qi, kj: (kj[p], qcol + 2)),
            pl.BlockSpec((GROUPS, 1, tq), lambda p, qi, kj: (0, 0, qi[p])),
            pl.BlockSpec((GROUPS, 1, tk), lambda p, qi, kj: (0, 0, kj[p])),
        ],
        out_specs=pl.BlockSpec((tq, BRANCH_W), lambda p, qi, kj: (qi[p], 0)),
        scratch_shapes=[pltpu.VMEM((GROUPS, tq, HEAD_DIM), F32), pltpu.VMEM((GROUPS, tq, HEAD_DIM), F32),
                        pltpu.VMEM((tq, BRANCH_W), F32)],
    )
    return pl.pallas_call(
        functools.partial(_attn_kernel, tq=tq, tk=tk, scale=HEAD_DIM ** -0.5),
        out_shape=jax.ShapeDtypeStruct((s, BRANCH_W), BF16),
        grid_spec=grid_spec,
        compiler_params=_params("arbitrary"),
        name="fox_attention",
    )(qi, kj, zmix, zmix, zmix, f_rows, f_rows)


def _mix_kernel(z_ref, zp_ref, cw_ref, ng_ref, ws_ref, sb_ref, pw_ref, ps_ref, o_ref, ext_ref, *, tb):
    i = pl.program_id(0)
    bw = BRANCH_W

    prev = zp_ref[...]
    prev = jnp.where(i > 0, prev, jnp.zeros_like(prev))
    ext_ref[0:HIST, 0:bw] = prev[:, bw:2 * bw] * prev[:, 2 * bw:3 * bw]
    ext_ref[HIST:HIST + tb, 0:bw] = z_ref[:, bw:2 * bw] * z_ref[:, 2 * bw:3 * bw]
    ext_ref[0:HIST, bw:2 * bw] = prev[:, OFF_C:OFF_C + bw]
    ext_ref[HIST:HIST + tb, bw:2 * bw] = z_ref[:, OFF_C:OFF_C + bw]
    conv = cw_ref[CONV_W - 1:CONV_W, :] * ext_ref[HIST:HIST + tb, 0:bw]
    for lag in range(1, CONV_W):
        conv = conv + cw_ref[CONV_W - 1 - lag:CONV_W - lag, :] * ext_ref[HIST - lag:HIST - lag + tb, 0:bw]
    o_ref[:, 0:bw] = (z_ref[:, 0:bw] * conv).astype(o_ref.dtype)

    u = _gelu_tanh(z_ref[:, OFF_B:OFF_B + bw])
    v = _gelu_tanh(z_ref[:, OFF_B + bw:OFF_B + 2 * bw])
    v = _rmsnorm_rows(v, ng_ref[...]).astype(BF16)
    for c in range(tb // CHUNK):
        rs = slice(c * CHUNK, (c + 1) * CHUNK)
        for g in range(GROUPS):
            cs = slice(g * HEAD_DIM, (g + 1) * HEAD_DIM)
            sv = jnp.dot(ws_ref[g], v[rs, cs], preferred_element_type=F32) + sb_ref[g]
            o_ref[rs, bw + g * HEAD_DIM:bw + (g + 1) * HEAD_DIM] = (u[rs, cs] * sv).astype(o_ref.dtype)

    t = (i * tb + 1 + lax.broadcasted_iota(jnp.int32, (tb, 1), 0)).astype(F32)
    for g, w in enumerate(POOL_WINDOWS):
        cs = slice(bw + g * HEAD_DIM, bw + (g + 1) * HEAD_DIM)
        cur = ext_ref[HIST:HIST + tb, cs]
        tot = cur
        for lag in range(1, w):
            tot = tot + ext_ref[HIST - lag:HIST - lag + tb, cs]
        pooled = (tot / jnp.minimum(t, float(w)) - cur).astype(BF16)
        y = jnp.dot(pooled, pw_ref[g], preferred_element_type=F32)
        y = y * ps_ref[:, g * HEAD_DIM:(g + 1) * HEAD_DIM]
        o_ref[:, 2 * bw + g * HEAD_DIM:2 * bw + (g + 1) * HEAD_DIM] = y.astype(o_ref.dtype)


def _mixers(zmix, conv_w, sgu_norm_g, sgu_w_tril, sgu_bias_b, pool_w, pool_scale, tb):
    s = zmix.shape[0]
    ncol = OFF_D
    hist_blocks = tb // HIST
    const2 = lambda i: (0, 0)
    const3 = lambda i: (0, 0, 0)
    return pl.pallas_call(
        functools.partial(_mix_kernel, tb=tb),
        out_shape=jax.ShapeDtypeStruct((s, 3 * BRANCH_W), BF16),
        grid=(s // tb,),
        in_specs=[pl.BlockSpec((tb, ncol), lambda i: (i, 0)),
                  pl.BlockSpec((HIST, ncol), lambda i: (jnp.maximum(i * hist_blocks - 1, 0), 0)),
                  pl.BlockSpec((CONV_W, BRANCH_W), const2),
                  pl.BlockSpec((1, BRANCH_W), const2),
                  pl.BlockSpec((GROUPS, CHUNK, CHUNK), const3),
                  pl.BlockSpec((GROUPS, CHUNK, HEAD_DIM), const3),
                  pl.BlockSpec((GROUPS, HEAD_DIM, HEAD_DIM), const3),
                  pl.BlockSpec((1, BRANCH_W), const2)],
        out_specs=pl.BlockSpec((tb, 3 * BRANCH_W), lambda i: (i, 0)),
        scratch_shapes=[pltpu.VMEM((HIST + tb, 2 * BRANCH_W), F32)],
        compiler_params=_params("parallel"),
        name="mixers",
    )(zmix, zmix, conv_w, sgu_norm_g, sgu_w_tril, sgu_bias_b, pool_w, pool_scale)


def _merge_kernel(abc_ref, od_ref, gates_ref, wb_ref, o_ref):
    d = o_ref.shape[1]
    acc = None
    for n in range(N_BRANCH):
        br = od_ref[...] if n == N_BRANCH - 1 else abc_ref[:, n * BRANCH_W:(n + 1) * BRANCH_W]
        y = jnp.dot(br, wb_ref[n], preferred_element_type=F32)
        term = gates_ref[:, n * d:(n + 1) * d].astype(F32) * y
        acc = term if acc is None else acc + term
    o_ref[...] = acc.astype(o_ref.dtype)


def _merge(abc, od, gates, w_branch, l, tm):
    s = abc.shape[0]
    d = w_branch.shape[3]
    return pl.pallas_call(
        _merge_kernel,
        out_shape=jax.ShapeDtypeStruct((s, d), BF16),
        grid=(s // tm,),
        in_specs=[pl.BlockSpec((tm, 3 * BRANCH_W), lambda i: (i, 0)),
                  pl.BlockSpec((tm, BRANCH_W), lambda i: (i, 0)),
                  pl.BlockSpec((tm, N_BRANCH * d), lambda i: (i, 0)),
                  _resident((None, N_BRANCH, BRANCH_W, d), lambda i: (l, 0, 0, 0))],
        out_specs=pl.BlockSpec((tm, d), lambda i: (i, 0)),
        compiler_params=_params("parallel"),
        name="merge",
    )(abc, od, gates, w_branch)


def _outproj_kernel(m_ref, w_ref, h_ref, g_ref, h1_ref, hn_ref, hnt_ref):
    h1 = h_ref[...] + jnp.dot(m_ref[...], w_ref[...], preferred_element_type=F32)
    h1_ref[...] = h1
    hn = _rmsnorm_rows(h1, g_ref[...])
    hn_ref[...] = hn.astype(hn_ref.dtype)
    hnt_ref[...] = jnp.transpose(hn).astype(hnt_ref.dtype)


def _outproj(merged, w_out, l, h, g2, tm):
    s, d = h.shape
    return pl.pallas_call(
        _outproj_kernel,
        out_shape=(jax.ShapeDtypeStruct((s, d), F32), jax.ShapeDtypeStruct((s, d), BF16),
                   jax.ShapeDtypeStruct((d, s), BF16)),
        grid=(s // tm,),
        in_specs=[pl.BlockSpec((tm, d), lambda i: (i, 0)),
                  _resident((None, d, d), lambda i: (l, 0, 0)),
                  pl.BlockSpec((tm, d), lambda i: (i, 0)),
                  pl.BlockSpec((1, d), lambda i: (0, 0))],
        out_specs=(pl.BlockSpec((tm, d), lambda i: (i, 0)),
                   pl.BlockSpec((tm, d), lambda i: (i, 0)),
                   pl.BlockSpec((d, tm), lambda i: (0, i))),
        compiler_params=_params("parallel"),
        name="outproj",
    )(merged, w_out, h, g2)


def _candidate_positions(t):
    row8 = lax.broadcasted_iota(jnp.int32, (8, t), 0)
    row16 = lax.broadcasted_iota(jnp.int32, (PEER_TOPK, t), 0)
    return jnp.concatenate(
        [row16.astype(F32)]
        + [(row8 + 16 * p).astype(F32) for p in range(1, 8)]
        + [((row8 + 8) * 16).astype(F32)], axis=0)


def _pair(x, y, op):
    return jnp.concatenate(
        [op(x[0:1], y)] + [op(x[p:p + 1], y[0:8]) for p in range(1, 8)] + [op(x[8:16], y[0:1])], axis=0)


def _top16(s, exact):
    t = s.shape[1]
    row16 = lax.broadcasted_iota(jnp.int32, (PEER_TOPK, t), 0)
    sub_iota = lax.broadcasted_iota(jnp.int32, s.shape, 0).astype(F32) if exact else None
    rank = jnp.full(s.shape, float(PEER_TOPK), F32)
    vals = jnp.zeros((PEER_TOPK, t), F32)
    work = s
    for r in range(PEER_TOPK):
        m = jnp.max(work, axis=0, keepdims=True)
        hit = work == m
        if exact:
            first = jnp.min(jnp.where(hit, sub_iota, float(PEER_KEYS)), axis=0, keepdims=True)
            hit = sub_iota == first
        rank = jnp.where(hit, float(r), rank)
        work = jnp.where(hit, -jnp.inf, work)
        vals = jnp.where(row16 == r, m, vals)
    count = jnp.sum(jnp.where(rank < float(PEER_TOPK), 1.0, 0.0), axis=0, keepdims=True)
    return rank, vals, count


def _select_pairs(cand, exact):
    pos = _candidate_positions(cand.shape[1]) if exact else None
    sel = jnp.zeros(cand.shape, F32)
    work = cand
    for _ in range(PEER_TOPK):
        m = jnp.max(work, axis=0, keepdims=True)
        hit = work == m
        if exact:
            first = jnp.min(jnp.where(hit, pos, 1e9), axis=0, keepdims=True)
            hit = pos == first
        sel = jnp.where(hit, 1.0, sel)
        work = jnp.where(hit, -jnp.inf, work)
    return sel, jnp.sum(sel, axis=0, keepdims=True)


def _retrieve_head(sc0, sc1, exact):
    t = sc0.shape[1]
    rank0, a, n0 = _top16(sc0, exact)
    rank1, b, n1 = _top16(sc1, exact)
    sel, n2 = _select_pairs(_pair(a, b, jnp.add), exact)
    wgt = _pair(jnp.exp(a - a[0:1]), jnp.exp(b - b[0:1]), jnp.multiply)
    z = jnp.sum(sel * wgt, axis=0, keepdims=True)
    row8 = lax.broadcasted_iota(jnp.int32, (8, t), 0)
    cnt_lo = jnp.zeros((8, t), F32)
    cnt_lo = jnp.where(row8 == 0, jnp.sum(sel[0:16], axis=0, keepdims=True), cnt_lo)
    for p in range(1, 8):
        cnt_lo = jnp.where(row8 == p, jnp.sum(sel[8 + 8 * p:16 + 8 * p], axis=0, keepdims=True), cnt_lo)
    cnt = jnp.concatenate([cnt_lo, sel[72:80]], axis=0)
    count = jnp.zeros((PEER_KEYS, t), F32)
    for r in range(PEER_TOPK):
        count = jnp.where(rank0 == float(r), cnt[r:r + 1], count)
    k = float(PEER_TOPK)
    ok = jnp.where((n0 == k) & (n1 == k) & (n2 == k), 1.0, 0.0)
    return rank1, jnp.exp(sc1 - b[0:1]), count, jnp.exp(sc0 - a[0:1]) / z, ok


def _retrieve_kernel(hn_ref, wq_ref, keys_ref, r1_ref, e1_ref, q_ref, c_ref, qs_ref):
    qf = jnp.dot(hn_ref[...], wq_ref[...], preferred_element_type=F32).astype(qs_ref.dtype)
    for n in range(2 * PEER_HEADS):
        qs_ref[n] = qf[:, n * PEER_HALF:(n + 1) * PEER_HALF]

    def head(h, carry):
        def scores(half):
            return lax.dot_general(keys_ref[h, half], qs_ref[2 * h + half],
                                   (((1,), (1,)), ((), ())), preferred_element_type=F32)

        def emit(exact):
            rank1, gate1, count, gate0, ok = _retrieve_head(scores(0), scores(1), exact)
            r1_ref[h] = rank1.astype(r1_ref.dtype)
            e1_ref[h] = gate1.astype(e1_ref.dtype)
            q_ref[h] = count
            c_ref[h] = gate0
            return ok

        ok = emit(False)

        @pl.when(jnp.min(ok) < 0.5)
        def _():
            emit(True)

        return carry

    lax.fori_loop(0, PEER_HEADS, head, 0)


def _retrieve(hn, wq, keys, l, tt):
    s, d = hn.shape
    shape = jax.ShapeDtypeStruct((PEER_HEADS, PEER_KEYS, s), F32)
    shape_lo = jax.ShapeDtypeStruct((PEER_HEADS, PEER_KEYS, s), BF16)
    out_spec = pl.BlockSpec((PEER_HEADS, PEER_KEYS, tt), lambda i: (0, 0, i))
    return pl.pallas_call(
        _retrieve_kernel,
        out_shape=(shape_lo, shape_lo, shape, shape),
        grid=(s // tt,),
        in_specs=[pl.BlockSpec((tt, d), lambda i: (i, 0)),
                  _resident((None,) + wq.shape[1:], lambda i: (l, 0, 0)),
                  _resident((None,) + keys.shape[1:], lambda i: (l, 0, 0, 0, 0))],
        out_specs=(out_spec, out_spec, out_spec, out_spec),
        scratch_shapes=[pltpu.VMEM((2 * PEER_HEADS, tt, PEER_HALF), BF16)],
        compiler_params=_params("parallel"),
        name="peer_retrieve",
    )(hn, wq, keys)


EXPERT_BLOCK = 8 * PEER_KEYS
EXPERT_CHAIN = 4 * PEER_KEYS


def _experts_kernel(hnt_ref, u_ref, vt_ref, r1_ref, e1_ref, q_ref, c_ref, o_ref):
    @pl.when(pl.program_id(1) == 0)
    def _():
        o_ref[...] = jnp.zeros(o_ref.shape, o_ref.dtype)

    tt = hnt_ref.shape[1]
    lo = e1_ref.dtype
    zero = jnp.zeros((), lo)
    n_i = EXPERT_CHAIN // PEER_KEYS
    for ch in range(EXPERT_BLOCK // EXPERT_CHAIN):
        rows = slice(ch * EXPERT_CHAIN, (ch + 1) * EXPERT_CHAIN)
        act = _gelu_tanh(jnp.dot(u_ref[rows], hnt_ref[...], preferred_element_type=F32))
        blocks = []
        for ii in range(n_i):
            i = ch * n_i + ii
            w = None
            for h in range(PEER_HEADS):
                qrow = q_ref[h, i:i + 1, :].astype(lo)
                crow = c_ref[h, i:i + 1, :].astype(lo)
                gate = jnp.where(r1_ref[h] < qrow, e1_ref[h] * crow, zero)
                w = gate if w is None else w + gate
            blocks.append(w.astype(F32) * act[ii * PEER_KEYS:(ii + 1) * PEER_KEYS])
        a_t = jnp.concatenate(blocks, axis=0).astype(vt_ref.dtype)
        o_ref[...] += jnp.dot(vt_ref[:, rows], a_t, preferred_element_type=F32)


def _experts(hn_t, u_tab, v_tab_t, l, r1, e1, q, c, tt):
    d, s = hn_t.shape
    n_exp = u_tab.shape[1]
    eb = EXPERT_BLOCK
    n_chain = eb // EXPERT_CHAIN
    return pl.pallas_call(
        _experts_kernel,
        out_shape=jax.ShapeDtypeStruct((d, s), F32),
        grid=(s // tt, n_exp // eb),
        in_specs=[pl.BlockSpec((d, tt), lambda i, e: (0, i), pipeline_mode=pl.Buffered(1)),
                  pl.BlockSpec((None, eb, d), lambda i, e: (l, e, 0)),
                  pl.BlockSpec((None, d, eb), lambda i, e: (l, 0, e)),
                  pl.BlockSpec((PEER_HEADS, PEER_KEYS, tt), lambda i, e: (0, 0, i),
                               pipeline_mode=pl.Buffered(1)),
                  pl.BlockSpec((PEER_HEADS, PEER_KEYS, tt), lambda i, e: (0, 0, i),
                               pipeline_mode=pl.Buffered(1)),
                  pl.BlockSpec((PEER_HEADS, eb // PEER_KEYS, tt), lambda i, e: (0, e, i)),
                  pl.BlockSpec((PEER_HEADS, eb // PEER_KEYS, tt), lambda i, e: (0, e, i))],
        out_specs=pl.BlockSpec((d, tt), lambda i, e: (0, i)),
        compiler_params=_params("parallel", "arbitrary"),
        name="peer_experts",
    )(hn_t, u_tab, v_tab_t, r1, e1, q, c)


def _tile(s, want):
    t = min(s, want)
    assert s % t == 0
    return t


def _layer(h, delta_t, l, w, p):
    s, d = h.shape
    h, xn = _addnorm(h, delta_t, p["norm1_g"], BF16, _tile(s, 512), True)
    zmix, fg = _inproj(xn, w["w_mix"], w["w_fg"], l, _tile(s, 1024), 1536)
    gates = _gates(xn, w["w_gate"], l, _tile(s, 1024), 1024)

    f_t = jnp.transpose(fg[:, :8])
    f_cum = _fcum(f_t, p["forget_b"])
    f_rows = f_cum[:GROUPS].reshape(GROUPS, 1, s)
    od = _attention(zmix, f_rows, _tile(s, 512), _tile(s, 512))

    abc = _mixers(zmix, p["conv_w"], p["sgu_norm_g"], p["sgu_w"], p["sgu_b"], p["pool_w"],
                  p["pool_scale"], _tile(s, 512))
    merged = _merge(abc, od, gates, w["w_branch"], l, _tile(s, 512))
    h1, hn, hn_t = _outproj(merged, w["w_out"], l, h, p["norm2_g"], _tile(s, 512))

    r1, e1, q, c = _retrieve(hn, w["peer_wq"], w["peer_keys"], l, _tile(s, 512))
    return h1, _experts(hn_t, w["peer_u"], w["peer_v_t"], l, r1, e1, q, c, _tile(s, 1024))


def kernel(x, norm1_g, w_in, conv_w, sgu_norm_g, sgu_w, sgu_b, pool_w, pool_scale, forget_b,
           w_branch, w_out, norm2_g, peer_wq, peer_keys, peer_u, peer_v, final_g):
    bsz, s, d = x.shape
    depth = w_in.shape[0]
    off_g = N_MIX + GROUPS
    assert w_in.shape[2] == off_g + N_BRANCH * d
    tril = jnp.tril(jnp.ones((CHUNK, CHUNK), dtype=bool))

    w = {
        "w_mix": w_in[:, :, :N_MIX].astype(BF16),
        "w_fg": jnp.pad(w_in[:, :, N_MIX:off_g], ((0, 0), (0, 0), (0, FG_PAD - GROUPS))).astype(BF16),
        "w_gate": w_in[:, :, off_g:].astype(BF16),
        "w_branch": w_branch.astype(BF16),
        "w_out": w_out.astype(BF16),
        "peer_wq": peer_wq.astype(BF16),
        "peer_keys": peer_keys.astype(BF16),
        "peer_u": peer_u.astype(BF16),
        "peer_v_t": jnp.transpose(peer_v.astype(BF16), (0, 2, 1)),
    }

    outs = []
    for b in range(bsz):
        h, delta = x[b], None
        for l in range(depth):
            p = {
                "norm1_g": norm1_g[l][None, :],
                "conv_w": conv_w[l],
                "sgu_norm_g": sgu_norm_g[l][None, :],
                "sgu_w": jnp.where(tril[None], sgu_w[l], 0.0).astype(BF16),
                "sgu_b": jnp.broadcast_to(sgu_b[l][:, :, None], (GROUPS, CHUNK, HEAD_DIM)),
                "pool_w": pool_w[l].astype(BF16),
                "pool_scale": pool_scale[l][None, :],
                "forget_b": jnp.pad(forget_b[l], (0, 8 - GROUPS))[:, None],
                "norm2_g": norm2_g[l][None, :],
            }
            h, delta = _layer(h, delta, l, w, p)
        outs.append(_addnorm(h, delta, final_g[None, :], F32, _tile(s, 512), False)[1])
    return jnp.stack(outs, axis=0)
```

```python
import functools
import math

import numpy as np
import jax
import jax.numpy as jnp
from jax import lax
from jax.experimental import pallas as pl
from jax.experimental.pallas import tpu as pltpu

F32 = jnp.float32
BF16 = jnp.bfloat16

HEAD_DIM = 128
GROUPS = 4
BRANCH_W = GROUPS * HEAD_DIM
N_BRANCH = 4
CONV_W = 3
CHUNK = 128
POOL_WINDOWS = (2, 4, 8, 16)
PEER_HEADS = 8
PEER_KEYS = 128
PEER_HALF = 128
PEER_TOPK = 16
EPS = 1e-6

OFF_B = 3 * BRANCH_W
OFF_C = OFF_B + 2 * BRANCH_W
OFF_D = OFF_C + BRANCH_W
N_MIX = OFF_D + 3 * BRANCH_W
HIST = 16
FG_PAD = 128

V7X_VMEM_LIMIT_BYTES = 56 * 1024 * 1024
NEG_BIG = -1e30
LOG2E = math.log2(math.e)


def _params(*sem, flags=None):
    return pltpu.CompilerParams(dimension_semantics=sem, vmem_limit_bytes=V7X_VMEM_LIMIT_BYTES, flags=flags)


def _gelu_tanh(x):
    return 0.5 * x * (1.0 + jnp.tanh(math.sqrt(2.0 / math.pi) * (x + 0.044715 * (x * x * x))))


def _rmsnorm_rows(x, g):
    return x * lax.rsqrt(jnp.mean(x * x, axis=-1, keepdims=True) + EPS) * g


def _resident(shape, index_map):
    return pl.BlockSpec(shape, index_map, pipeline_mode=pl.Buffered(1))


def _addnorm_kernel(*refs, has_delta, emit_stream):
    h_ref, refs = refs[0], refs[1:]
    h = h_ref[...]
    if has_delta:
        h = h + jnp.transpose(refs[0][...])
        refs = refs[1:]
    g_ref, refs = refs[0], refs[1:]
    if emit_stream:
        refs[0][...] = h
        refs = refs[1:]
    refs[0][...] = _rmsnorm_rows(h, g_ref[...]).astype(refs[0].dtype)


def _addnorm(h, delta_t, g, out_dtype, tm, want_stream):
    s, d = h.shape
    has_delta = delta_t is not None
    row_spec = pl.BlockSpec((tm, d), lambda i: (i, 0))
    in_specs = [row_spec] + ([pl.BlockSpec((d, tm), lambda i: (0, i))] if has_delta else []) + [
        pl.BlockSpec((1, d), lambda i: (0, 0))]
    out_shape = [jax.ShapeDtypeStruct((s, d), out_dtype)]
    out_specs = [row_spec]
    emit_stream = has_delta and want_stream
    if emit_stream:
        out_shape.insert(0, jax.ShapeDtypeStruct((s, d), F32))
        out_specs.insert(0, row_spec)
    outs = pl.pallas_call(
        functools.partial(_addnorm_kernel, has_delta=has_delta, emit_stream=emit_stream),
        out_shape=tuple(out_shape),
        grid=(s // tm,),
        in_specs=in_specs,
        out_specs=tuple(out_specs),
        compiler_params=_params("parallel"),
        name="addnorm",
    )(*((h, delta_t) if has_delta else (h,)), g)
    return outs if emit_stream else (h, outs[0])


def _inproj_kernel(xn_ref, w_ref, wfg_ref, z_ref, fg_ref):
    @pl.when(pl.program_id(1) == 0)
    def _():
        fg_ref[...] = jnp.dot(xn_ref[...], wfg_ref[...], preferred_element_type=F32)

    z_ref[...] = jnp.dot(xn_ref[...], w_ref[...], preferred_element_type=F32)


def _inproj(xn, w_mix, w_fg, l, tm, tn):
    s, d = xn.shape
    n = w_mix.shape[2]
    return pl.pallas_call(
        _inproj_kernel,
        out_shape=(jax.ShapeDtypeStruct((s, n), F32), jax.ShapeDtypeStruct((s, FG_PAD), F32)),
        grid=(s // tm, n // tn),
        in_specs=[pl.BlockSpec((tm, d), lambda i, j: (i, 0)),
                  pl.BlockSpec((None, d, tn), lambda i, j: (l, 0, j)),
                  pl.BlockSpec((None, d, FG_PAD), lambda i, j: (l, 0, 0))],
        out_specs=(pl.BlockSpec((tm, tn), lambda i, j: (i, j)),
                   pl.BlockSpec((tm, FG_PAD), lambda i, j: (i, 0))),
        compiler_params=_params("parallel", "arbitrary"),
        name="inproj",
    )(xn, w_mix, w_fg)


def _gates_kernel(xn_ref, w_ref, o_ref):
    z = jnp.dot(xn_ref[...], w_ref[...], preferred_element_type=F32)
    o_ref[...] = (1.0 / (1.0 + jnp.exp(-z))).astype(o_ref.dtype)


def _gates(xn, w_gate, l, tm, tn):
    s, d = xn.shape
    n = w_gate.shape[2]
    return pl.pallas_call(
        _gates_kernel,
        out_shape=jax.ShapeDtypeStruct((s, n), BF16),
        grid=(s // tm, n // tn),
        in_specs=[pl.BlockSpec((tm, d), lambda i, j: (i, 0)),
                  pl.BlockSpec((None, d, tn), lambda i, j: (l, 0, j))],
        out_specs=pl.BlockSpec((tm, tn), lambda i, j: (i, j)),
        compiler_params=_params("parallel", "arbitrary"),
        name="gates",
    )(xn, w_gate)


CUM_BLOCK = 256


def _fcum_kernel(x_ref, b_ref, tri_ref, o_ref):
    s = x_ref.shape[1]
    carry = jnp.zeros((x_ref.shape[0], 1), F32)
    for c in range(s // CUM_BLOCK):
        sl = slice(c * CUM_BLOCK, (c + 1) * CUM_BLOCK)
        x = x_ref[:, sl] + b_ref[...]
        ls = jnp.minimum(x, 0.0) - jnp.log1p(jnp.exp(-jnp.abs(x)))
        y = jnp.dot(ls, tri_ref[...], precision=lax.Precision.HIGHEST,
                    preferred_element_type=F32) + carry
        o_ref[:, sl] = y
        carry = y[:, CUM_BLOCK - 1:CUM_BLOCK]


def _fcum(f_t, b_col):
    tri = jnp.asarray(np.triu(np.ones((CUM_BLOCK, CUM_BLOCK), np.float32)))
    return pl.pallas_call(
        _fcum_kernel,
        out_shape=jax.ShapeDtypeStruct(f_t.shape, F32),
        compiler_params=pltpu.CompilerParams(vmem_limit_bytes=V7X_VMEM_LIMIT_BYTES),
        name="fcum",
    )(f_t, b_col, tri)


def _attn_kernel(qi_ref, kj_ref, q_ref, k_ref, v_ref, fq_ref, fk_ref, o_ref, m_ref, l_ref, acc_ref,
                 *, tq, tk, scale):
    p = pl.program_id(0)
    qi = qi_ref[p]
    kj = kj_ref[p]

    @pl.when(kj == 0)
    def _():
        m_ref[...] = jnp.full(m_ref.shape, NEG_BIG, F32)
        l_ref[...] = jnp.zeros(l_ref.shape, F32)
        acc_ref[...] = jnp.zeros(acc_ref.shape, F32)

    def step(masked):
        if masked:
            rows = qi * tq + lax.broadcasted_iota(jnp.int32, (tq, tk), 0)
            cols = kj * tk + lax.broadcasted_iota(jnp.int32, (tq, tk), 1)
            keep = rows >= cols
        ones = jnp.ones((tk, HEAD_DIM), BF16)
        for h in range(GROUPS):
            cs = slice(h * HEAD_DIM, (h + 1) * HEAD_DIM)
            q = q_ref[:, cs].astype(BF16)
            k = k_ref[:, cs].astype(BF16)
            s = lax.dot_general(q, k, (((1,), (1,)), ((), ())), preferred_element_type=F32) * (scale * LOG2E)
            fq = fq_ref[h]
            s = s + (fq[:, 0:1] - fk_ref[h]) * LOG2E
            if masked:
                s = jnp.where(keep, s, NEG_BIG)
            m_prev = m_ref[h]
            m_new = jnp.maximum(m_prev, jnp.max(s, axis=-1, keepdims=True))
            alpha = jnp.exp2(m_prev - m_new)
            pr = jnp.exp2(s - jnp.concatenate([m_new] * (tk // HEAD_DIM), axis=1)).astype(BF16)
            pv = jnp.dot(pr, jnp.concatenate([v_ref[:, cs].astype(BF16), ones], axis=1),
                         preferred_element_type=F32)
            l_ref[h] = alpha * l_ref[h] + pv[:, HEAD_DIM:]
            acc_ref[:, cs] = alpha * acc_ref[:, cs] + pv[:, :HEAD_DIM]
            m_ref[h] = m_new

    last = (qi * tq + tq - 1) // tk
    first_masked = (qi * tq) // tk

    @pl.when(kj < first_masked)
    def _():
        step(False)

    @pl.when(kj >= first_masked)
    def _():
        step(True)

    @pl.when(kj == last)
    def _():
        for h in range(GROUPS):
            cs = slice(h * HEAD_DIM, (h + 1) * HEAD_DIM)
            o_ref[:, cs] = (acc_ref[:, cs] / l_ref[h]).astype(o_ref.dtype)


def _attention(zmix, f_rows, tq, tk):
    s = zmix.shape[0]
    nq = s // tq
    pairs = [(i, j) for i in range(nq) for j in range((i * tq + tq - 1) // tk + 1)]
    qi = jnp.asarray(np.array([p[0] for p in pairs], np.int32))
    kj = jnp.asarray(np.array([p[1] for p in pairs], np.int32))
    qcol = OFF_D // BRANCH_W
    grid_spec = pltpu.PrefetchScalarGridSpec(
        num_scalar_prefetch=2,
        grid=(len(pairs),),
        in_specs=[
            pl.BlockSpec((tq, BRANCH_W), lambda p, qi, kj: (qi[p], qcol)),
            pl.BlockSpec((tk, BRANCH_W), lambda p, qi, kj: (kj[p], qcol + 1)),
            pl.BlockSpec((tk, BRANCH_W), lambda p, qi, kj: (kj[p], qcol + 2)),
            pl.BlockSpec((GROUPS, 1, tq), lambda p, qi, kj: (0, 0, qi[p])),
            pl.BlockSpec((GROUPS, 1, tk), lambda p, qi, kj: (0, 0, kj[p])),
        ],
        out_specs=pl.BlockSpec((tq, BRANCH_W), lambda p, qi, kj: (qi[p], 0)),
        scratch_shapes=[pltpu.VMEM((GROUPS, tq, HEAD_DIM), F32), pltpu.VMEM((GROUPS, tq, HEAD_DIM), F32),
                        pltpu.VMEM((tq, BRANCH_W), F32)],
    )
    return pl.pallas_call(
        functools.partial(_attn_kernel, tq=tq, tk=tk, scale=HEAD_DIM ** -0.5),
        out_shape=jax.ShapeDtypeStruct((s, BRANCH_W), BF16),
        grid_spec=grid_spec,
        compiler_params=_params("arbitrary"),
        name="fox_attention",
    )(qi, kj, zmix, zmix, zmix, f_rows, f_rows)


def _mix_kernel(z_ref, zp_ref, cw_ref, ng_ref, ws_ref, sb_ref, pw_ref, ps_ref, o_ref, ext_ref, *, tb):
    i = pl.program_id(0)
    bw = BRANCH_W

    prev = zp_ref[...]
    prev = jnp.where(i > 0, prev, jnp.zeros_like(prev))
    ext_ref[0:HIST, 0:bw] = prev[:, bw:2 * bw] * prev[:, 2 * bw:3 * bw]
    ext_ref[HIST:HIST + tb, 0:bw] = z_ref[:, bw:2 * bw] * z_ref[:, 2 * bw:3 * bw]
    ext_ref[0:HIST, bw:2 * bw] = prev[:, OFF_C:OFF_C + bw]
    ext_ref[HIST:HIST + tb, bw:2 * bw] = z_ref[:, OFF_C:OFF_C + bw]
    conv = cw_ref[CONV_W - 1:CONV_W, :] * ext_ref[HIST:HIST + tb, 0:bw]
    for lag in range(1, CONV_W):
        conv = conv + cw_ref[CONV_W - 1 - lag:CONV_W - lag, :] * ext_ref[HIST - lag:HIST - lag + tb, 0:bw]
    o_ref[:, 0:bw] = (z_ref[:, 0:bw] * conv).astype(o_ref.dtype)

    u = _gelu_tanh(z_ref[:, OFF_B:OFF_B + bw])
    v = _gelu_tanh(z_ref[:, OFF_B + bw:OFF_B + 2 * bw])
    v = _rmsnorm_rows(v, ng_ref[...]).astype(BF16)
    for c in range(tb // CHUNK):
        rs = slice(c * CHUNK, (c + 1) * CHUNK)
        for g in range(GROUPS):
            cs = slice(g * HEAD_DIM, (g + 1) * HEAD_DIM)
            sv = jnp.dot(ws_ref[g], v[rs, cs], preferred_element_type=F32) + sb_ref[g]
            o_ref[rs, bw + g * HEAD_DIM:bw + (g + 1) * HEAD_DIM] = (u[rs, cs] * sv).astype(o_ref.dtype)

    t = (i * tb + 1 + lax.broadcasted_iota(jnp.int32, (tb, 1), 0)).astype(F32)
    for g, w in enumerate(POOL_WINDOWS):
        cs = slice(bw + g * HEAD_DIM, bw + (g + 1) * HEAD_DIM)
        cur = ext_ref[HIST:HIST + tb, cs]
        tot = cur
        for lag in range(1, w):
            tot = tot + ext_ref[HIST - lag:HIST - lag + tb, cs]
        pooled = (tot / jnp.minimum(t, float(w)) - cur).astype(BF16)
        y = jnp.dot(pooled, pw_ref[g], preferred_element_type=F32)
        y = y * ps_ref[:, g * HEAD_DIM:(g + 1) * HEAD_DIM]
        o_ref[:, 2 * bw + g * HEAD_DIM:2 * bw + (g + 1) * HEAD_DIM] = y.astype(o_ref.dtype)


def _mixers(zmix, conv_w, sgu_norm_g, sgu_w_tril, sgu_bias_b, pool_w, pool_scale, tb):
    s = zmix.shape[0]
    ncol = OFF_D
    hist_blocks = tb // HIST
    const2 = lambda i: (0, 0)
    const3 = lambda i: (0, 0, 0)
    return pl.pallas_call(
        functools.partial(_mix_kernel, tb=tb),
        out_shape=jax.ShapeDtypeStruct((s, 3 * BRANCH_W), BF16),
        grid=(s // tb,),
        in_specs=[pl.BlockSpec((tb, ncol), lambda i: (i, 0)),
                  pl.BlockSpec((HIST, ncol), lambda i: (jnp.maximum(i * hist_blocks - 1, 0), 0)),
                  pl.BlockSpec((CONV_W, BRANCH_W), const2),
                  pl.BlockSpec((1, BRANCH_W), const2),
                  pl.BlockSpec((GROUPS, CHUNK, CHUNK), const3),
                  pl.BlockSpec((GROUPS, CHUNK, HEAD_DIM), const3),
                  pl.BlockSpec((GROUPS, HEAD_DIM, HEAD_DIM), const3),
                  pl.BlockSpec((1, BRANCH_W), const2)],
        out_specs=pl.BlockSpec((tb, 3 * BRANCH_W), lambda i: (i, 0)),
        scratch_shapes=[pltpu.VMEM((HIST + tb, 2 * BRANCH_W), F32)],
        compiler_params=_params("parallel"),
        name="mixers",
    )(zmix, zmix, conv_w, sgu_norm_g, sgu_w_tril, sgu_bias_b, pool_w, pool_scale)


def _merge_kernel(abc_ref, od_ref, gates_ref, wb_ref, o_ref):
    d = o_ref.shape[1]
    acc = None
    for n in range(N_BRANCH):
        br = od_ref[...] if n == N_BRANCH - 1 else abc_ref[:, n * BRANCH_W:(n + 1) * BRANCH_W]
        y = jnp.dot(br, wb_ref[n], preferred_element_type=F32)
        term = gates_ref[:, n * d:(n + 1) * d].astype(F32) * y
        acc = term if acc is None else acc + term
    o_ref[...] = acc.astype(o_ref.dtype)


def _merge(abc, od, gates, w_branch, l, tm):
    s = abc.shape[0]
    d = w_branch.shape[3]
    return pl.pallas_call(
        _merge_kernel,
        out_shape=jax.ShapeDtypeStruct((s, d), BF16),
        grid=(s // tm,),
        in_specs=[pl.BlockSpec((tm, 3 * BRANCH_W), lambda i: (i, 0)),
                  pl.BlockSpec((tm, BRANCH_W), lambda i: (i, 0)),
                  pl.BlockSpec((tm, N_BRANCH * d), lambda i: (i, 0)),
                  _resident((None, N_BRANCH, BRANCH_W, d), lambda i: (l, 0, 0, 0))],
        out_specs=pl.BlockSpec((tm, d), lambda i: (i, 0)),
        compiler_params=_params("parallel"),
        name="merge",
    )(abc, od, gates, w_branch)


def _outproj_kernel(m_ref, w_ref, h_ref, g_ref, h1_ref, hn_ref, hnt_ref):
    h1 = h_ref[...] + jnp.dot(m_ref[...], w_ref[...], preferred_element_type=F32)
    h1_ref[...] = h1
    hn = _rmsnorm_rows(h1, g_ref[...])
    hn_ref[...] = hn.astype(hn_ref.dtype)
    hnt_ref[...] = jnp.transpose(hn).astype(hnt_ref.dtype)


def _outproj(merged, w_out, l, h, g2, tm):
    s, d = h.shape
    return pl.pallas_call(
        _outproj_kernel,
        out_shape=(jax.ShapeDtypeStruct((s, d), F32), jax.ShapeDtypeStruct((s, d), BF16),
                   jax.ShapeDtypeStruct((d, s), BF16)),
        grid=(s // tm,),
        in_specs=[pl.BlockSpec((tm, d), lambda i: (i, 0)),
                  _resident((None, d, d), lambda i: (l, 0, 0)),
                  pl.BlockSpec((tm, d), lambda i: (i, 0)),
                  pl.BlockSpec((1, d), lambda i: (0, 0))],
        out_specs=(pl.BlockSpec((tm, d), lambda i: (i, 0)),
                   pl.BlockSpec((tm, d), lambda i: (i, 0)),
                   pl.BlockSpec((d, tm), lambda i: (0, i))),
        compiler_params=_params("parallel"),
        name="outproj",
    )(merged, w_out, h, g2)


def _candidate_positions(t):
    row8 = lax.broadcasted_iota(jnp.int32, (8, t), 0)
    row16 = lax.broadcasted_iota(jnp.int32, (PEER_TOPK, t), 0)
    return jnp.concatenate(
        [row16.astype(F32)]
        + [(row8 + 16 * p).astype(F32) for p in range(1, 8)]
        + [((row8 + 8) * 16).astype(F32)], axis=0)


def _pair(x, y, op):
    return jnp.concatenate(
        [op(x[0:1], y)] + [op(x[p:p + 1], y[0:8]) for p in range(1, 8)] + [op(x[8:16], y[0:1])], axis=0)


def _top16(s, exact):
    t = s.shape[1]
    row16 = lax.broadcasted_iota(jnp.int32, (PEER_TOPK, t), 0)
    sub_iota = lax.broadcasted_iota(jnp.int32, s.shape, 0).astype(F32) if exact else None
    rank = jnp.full(s.shape, float(PEER_TOPK), F32)
    vals = jnp.zeros((PEER_TOPK, t), F32)
    work = s
    for r in range(PEER_TOPK):
        m = jnp.max(work, axis=0, keepdims=True)
        hit = work == m
        if exact:
            first = jnp.min(jnp.where(hit, sub_iota, float(PEER_KEYS)), axis=0, keepdims=True)
            hit = sub_iota == first
        rank = jnp.where(hit, float(r), rank)
        work = jnp.where(hit, -jnp.inf, work)
        vals = jnp.where(row16 == r, m, vals)
    count = jnp.sum(jnp.where(rank < float(PEER_TOPK), 1.0, 0.0), axis=0, keepdims=True)
    return rank, vals, count


def _select_pairs(cand, exact):
    pos = _candidate_positions(cand.shape[1]) if exact else None
    sel = jnp.zeros(cand.shape, F32)
    work = cand
    for _ in range(PEER_TOPK):
        m = jnp.max(work, axis=0, keepdims=True)
        hit = work == m
        if exact:
            first = jnp.min(jnp.where(hit, pos, 1e9), axis=0, keepdims=True)
            hit = pos == first
        sel = jnp.where(hit, 1.0, sel)
        work = jnp.where(hit, -jnp.inf, work)
    return sel, jnp.sum(sel, axis=0, keepdims=True)


def _retrieve_head(sc0, sc1, exact):
    t = sc0.shape[1]
    rank0, a, n0 = _top16(sc0, exact)
    rank1, b, n1 = _top16(sc1, exact)
    sel, n2 = _select_pairs(_pair(a, b, jnp.add), exact)
    wgt = _pair(jnp.exp(a - a[0:1]), jnp.exp(b - b[0:1]), jnp.multiply)
    z = jnp.sum(sel * wgt, axis=0, keepdims=True)
    row8 = lax.broadcasted_iota(jnp.int32, (8, t), 0)
    cnt_lo = jnp.zeros((8, t), F32)
    cnt_lo = jnp.where(row8 == 0, jnp.sum(sel[0:16], axis=0, keepdims=True), cnt_lo)
    for p in range(1, 8):
        cnt_lo = jnp.where(row8 == p, jnp.sum(sel[8 + 8 * p:16 + 8 * p], axis=0, keepdims=True), cnt_lo)
    cnt = jnp.concatenate([cnt_lo, sel[72:80]], axis=0)
    count = jnp.zeros((PEER_KEYS, t), F32)
    for r in range(PEER_TOPK):
        count = jnp.where(rank0 == float(r), cnt[r:r + 1], count)
    k = float(PEER_TOPK)
    ok = jnp.where((n0 == k) & (n1 == k) & (n2 == k), 1.0, 0.0)
    return rank1, jnp.exp(sc1 - b[0:1]), count, jnp.exp(sc0 - a[0:1]) / z, ok


def _retrieve_kernel(hn_ref, wq_ref, keys_ref, r1_ref, e1_ref, q_ref, c_ref, qs_ref):
    qf = jnp.dot(hn_ref[...], wq_ref[...], preferred_element_type=F32).astype(qs_ref.dtype)
    for n in range(2 * PEER_HEADS):
        qs_ref[n] = qf[:, n * PEER_HALF:(n + 1) * PEER_HALF]

    def head(h, carry):
        def scores(half):
            return lax.dot_general(keys_ref[h, half], qs_ref[2 * h + half],
                                   (((1,), (1,)), ((), ())), preferred_element_type=F32)

        def emit(exact):
            rank1, gate1, count, gate0, ok = _retrieve_head(scores(0), scores(1), exact)
            r1_ref[h] = rank1.astype(r1_ref.dtype)
            e1_ref[h] = gate1.astype(e1_ref.dtype)
            q_ref[h] = count
            c_ref[h] = gate0
            return ok

        ok = emit(False)

        @pl.when(jnp.min(ok) < 0.5)
        def _():
            emit(True)

        return carry

    lax.fori_loop(0, PEER_HEADS, head, 0)


def _retrieve(hn, wq, keys, l, tt):
    s, d = hn.shape
    shape = jax.ShapeDtypeStruct((PEER_HEADS, PEER_KEYS, s), F32)
    shape_lo = jax.ShapeDtypeStruct((PEER_HEADS, PEER_KEYS, s), BF16)
    out_spec = pl.BlockSpec((PEER_HEADS, PEER_KEYS, tt), lambda i: (0, 0, i))
    return pl.pallas_call(
        _retrieve_kernel,
        out_shape=(shape_lo, shape_lo, shape, shape),
        grid=(s // tt,),
        in_specs=[pl.BlockSpec((tt, d), lambda i: (i, 0)),
                  _resident((None,) + wq.shape[1:], lambda i: (l, 0, 0)),
                  _resident((None,) + keys.shape[1:], lambda i: (l, 0, 0, 0, 0))],
        out_specs=(out_spec, out_spec, out_spec, out_spec),
        scratch_shapes=[pltpu.VMEM((2 * PEER_HEADS, tt, PEER_HALF), BF16)],
        compiler_params=_params("parallel"),
        name="peer_retrieve",
    )(hn, wq, keys)


EXPERT_TILE = 4 * PEER_KEYS
TILE_ROWS = EXPERT_TILE // PEER_KEYS
TOKEN_PIECE = 256


def _gate_tile(act, r1_ref, e1_ref, q_ref, c_ref, row0, cols):
    lo = e1_ref.dtype
    zero = jnp.zeros((), lo)
    blocks = []
    for ii in range(TILE_ROWS):
        i = row0 + ii
        w = None
        for h in range(PEER_HEADS):
            qrow = q_ref[h, i:i + 1, cols].astype(lo)
            crow = c_ref[h, i:i + 1, cols].astype(lo)
            gate = jnp.where(r1_ref[h, :, cols] < qrow, e1_ref[h, :, cols] * crow, zero)
            w = gate if w is None else w + gate
        blocks.append(w.astype(F32) * _gelu_tanh(act[ii * PEER_KEYS:(ii + 1) * PEER_KEYS]))
    return jnp.concatenate(blocks, axis=0)


def _experts_kernel(hnt_ref, u_ref, vt_ref, r1_ref, e1_ref, qa_ref, ca_ref, qb_ref, cb_ref, o_ref,
                    act_ref, a_ref):
    g = pl.program_id(1)
    last = pl.num_programs(1) - 1
    even, odd = slice(0, EXPERT_TILE), slice(EXPERT_TILE, 2 * EXPERT_TILE)

    @pl.when(g == 0)
    def _():
        o_ref[...] = jnp.zeros(o_ref.shape, o_ref.dtype)
        a_ref[0] = jnp.zeros(a_ref.shape[1:], a_ref.dtype)
        act_ref[1] = jnp.zeros(act_ref.shape[1:], act_ref.dtype)

    tt = hnt_ref.shape[1]
    pieces = [slice(c, c + TOKEN_PIECE) for c in range(0, tt, TOKEN_PIECE)]

    def first_matmul(slot, rows, tc):
        act_ref[slot, :, tc] = jnp.dot(u_ref[rows], hnt_ref[:, tc], preferred_element_type=F32)

    def gates(slot, q_ref, c_ref, row0, tc):
        a_ref[slot, :, tc] = _gate_tile(act_ref[slot, :, tc], r1_ref, e1_ref, q_ref, c_ref, row0,
                                        tc).astype(a_ref.dtype)

    def second_matmul(slot, cols, tc):
        o_ref[:, tc] += jnp.dot(vt_ref[:, cols], a_ref[slot, :, tc], preferred_element_type=F32)

    @pl.when(g < last)
    def _():
        for tc in pieces:
            second_matmul(0, even, tc)
            gates(1, qa_ref, ca_ref, TILE_ROWS, tc)
            first_matmul(0, even, tc)
        for tc in pieces:
            second_matmul(1, odd, tc)
            gates(0, qb_ref, cb_ref, 0, tc)
            first_matmul(1, odd, tc)

    @pl.when(g == last)
    def _():
        for tc in pieces:
            second_matmul(0, even, tc)
            gates(1, qa_ref, ca_ref, TILE_ROWS, tc)
        for tc in pieces:
            second_matmul(1, odd, tc)


def _experts(hn_t, u_tab, v_tab_t, l, r1, e1, q, c, tt):
    d, s = hn_t.shape
    n_exp = u_tab.shape[1]
    eb = 2 * EXPERT_TILE
    n_blocks = n_exp // eb
    behind = lambda e: jnp.maximum(e - 1, 0)
    ahead = lambda e: jnp.minimum(e, n_blocks - 1)
    sel_spec = pl.BlockSpec((PEER_HEADS, PEER_KEYS, tt), lambda i, e: (0, 0, i), pipeline_mode=pl.Buffered(1))
    row_block = (PEER_HEADS, eb // PEER_KEYS, tt)
    return pl.pallas_call(
        _experts_kernel,
        out_shape=jax.ShapeDtypeStruct((d, s), F32),
        grid=(s // tt, n_blocks + 1),
        in_specs=[pl.BlockSpec((d, tt), lambda i, e: (0, i), pipeline_mode=pl.Buffered(1)),
                  pl.BlockSpec((None, eb, d), lambda i, e: (l, ahead(e), 0)),
                  pl.BlockSpec((None, d, eb), lambda i, e: (l, 0, behind(e))),
                  sel_spec, sel_spec,
                  pl.BlockSpec(row_block, lambda i, e: (0, behind(e), i)),
                  pl.BlockSpec(row_block, lambda i, e: (0, behind(e), i)),
                  pl.BlockSpec(row_block, lambda i, e: (0, ahead(e), i)),
                  pl.BlockSpec(row_block, lambda i, e: (0, ahead(e), i))],
        out_specs=pl.BlockSpec((d, tt), lambda i, e: (0, i)),
        scratch_shapes=[pltpu.VMEM((2, EXPERT_TILE, tt), F32), pltpu.VMEM((2, EXPERT_TILE, tt), BF16)],
        compiler_params=_params("parallel", "arbitrary"),
        name="peer_experts",
    )(hn_t, u_tab, v_tab_t, r1, e1, q, c, q, c)


def _tile(s, want):
    t = min(s, want)
    assert s % t == 0
    return t


def _layer(h, delta_t, l, w, p):
    s, d = h.shape
    h, xn = _addnorm(h, delta_t, p["norm1_g"], BF16, _tile(s, 512), True)
    zmix, fg = _inproj(xn, w["w_mix"], w["w_fg"], l, _tile(s, 1024), 1536)
    gates = _gates(xn, w["w_gate"], l, _tile(s, 1024), 1024)

    f_t = jnp.transpose(fg[:, :8])
    f_cum = _fcum(f_t, p["forget_b"])
    f_rows = f_cum[:GROUPS].reshape(GROUPS, 1, s)
    od = _attention(zmix, f_rows, _tile(s, 512), _tile(s, 512))

    abc = _mixers(zmix, p["conv_w"], p["sgu_norm_g"], p["sgu_w"], p["sgu_b"], p["pool_w"],
                  p["pool_scale"], _tile(s, 512))
    merged = _merge(abc, od, gates, w["w_branch"], l, _tile(s, 512))
    h1, hn, hn_t = _outproj(merged, w["w_out"], l, h, p["norm2_g"], _tile(s, 512))

    r1, e1, q, c = _retrieve(hn, w["peer_wq"], w["peer_keys"], l, _tile(s, 512))
    return h1, _experts(hn_t, w["peer_u"], w["peer_v_t"], l, r1, e1, q, c, _tile(s, 1024))


def kernel(x, norm1_g, w_in, conv_w, sgu_norm_g, sgu_w, sgu_b, pool_w, pool_scale, forget_b,
           w_branch, w_out, norm2_g, peer_wq, peer_keys, peer_u, peer_v, final_g):
    bsz, s, d = x.shape
    depth = w_in.shape[0]
    off_g = N_MIX + GROUPS
    assert w_in.shape[2] == off_g + N_BRANCH * d
    tril = jnp.tril(jnp.ones((CHUNK, CHUNK), dtype=bool))

    w = {
        "w_mix": w_in[:, :, :N_MIX].astype(BF16),
        "w_fg": jnp.pad(w_in[:, :, N_MIX:off_g], ((0, 0), (0, 0), (0, FG_PAD - GROUPS))).astype(BF16),
        "w_gate": w_in[:, :, off_g:].astype(BF16),
        "w_branch": w_branch.astype(BF16),
        "w_out": w_out.astype(BF16),
        "peer_wq": peer_wq.astype(BF16),
        "peer_keys": peer_keys.astype(BF16),
        "peer_u": peer_u.astype(BF16),
        "peer_v_t": jnp.transpose(peer_v.astype(BF16), (0, 2, 1)),
    }

    outs = []
    for b in range(bsz):
        h, delta = x[b], None
        for l in range(depth):
            p = {
                "norm1_g": norm1_g[l][None, :],
                "conv_w": conv_w[l],
                "sgu_norm_g": sgu_norm_g[l][None, :],
                "sgu_w": jnp.where(tril[None], sgu_w[l], 0.0).astype(BF16),
                "sgu_b": jnp.broadcast_to(sgu_b[l][:, :, None], (GROUPS, CHUNK, HEAD_DIM)),
                "pool_w": pool_w[l].astype(BF16),
                "pool_scale": pool_scale[l][None, :],
                "forget_b": jnp.pad(forget_b[l], (0, 8 - GROUPS))[:, None],
                "norm2_g": norm2_g[l][None, :],
            }
            h, delta = _layer(h, delta, l, w, p)
        outs.append(_addnorm(h, delta, final_g[None, :], F32, _tile(s, 512), False)[1])
    return jnp.stack(outs, axis=0)
```

```python
import functools
import math

import numpy as np
import jax
import jax.numpy as jnp
from jax import lax
from jax.experimental import pallas as pl
from jax.experimental.pallas import tpu as pltpu

F32 = jnp.float32
BF16 = jnp.bfloat16

HEAD_DIM = 128
GROUPS = 4
BRANCH_W = GROUPS * HEAD_DIM
N_BRANCH = 4
CONV_W = 3
CHUNK = 128
POOL_WINDOWS = (2, 4, 8, 16)
PEER_HEADS = 8
PEER_KEYS = 128
PEER_HALF = 128
PEER_TOPK = 16
EPS = 1e-6

OFF_B = 3 * BRANCH_W
OFF_C = OFF_B + 2 * BRANCH_W
OFF_D = OFF_C + BRANCH_W
N_MIX = OFF_D + 3 * BRANCH_W
HIST = 16
FG_PAD = 128

V7X_VMEM_LIMIT_BYTES = 56 * 1024 * 1024
NEG_BIG = -1e30
LOG2E = math.log2(math.e)


def _params(*sem, flags=None):
    return pltpu.CompilerParams(dimension_semantics=sem, vmem_limit_bytes=V7X_VMEM_LIMIT_BYTES, flags=flags)


def _gelu_tanh(x):
    return 0.5 * x * (1.0 + jnp.tanh(math.sqrt(2.0 / math.pi) * (x + 0.044715 * (x * x * x))))


def _rmsnorm_rows(x, g):
    return x * lax.rsqrt(jnp.mean(x * x, axis=-1, keepdims=True) + EPS) * g


def _resident(shape, index_map):
    return pl.BlockSpec(shape, index_map, pipeline_mode=pl.Buffered(1))


def _split_kernel(w_ref, mix_ref, fg_ref, gate_ref):
    mix_ref[...] = w_ref[:, :N_MIX].astype(mix_ref.dtype)
    lane = lax.broadcasted_iota(jnp.int32, fg_ref.shape, 1)
    fg_ref[...] = jnp.where(lane < GROUPS, w_ref[:, N_MIX:N_MIX + FG_PAD], 0.0).astype(fg_ref.dtype)
    gate_ref[...] = w_ref[:, N_MIX + GROUPS:].astype(gate_ref.dtype)


def _split_w_in(w_in, rows):
    depth, d, cols = w_in.shape
    n_gate = cols - N_MIX - GROUPS
    out = lambda n: pl.BlockSpec((None, rows, n), lambda l, i: (l, i, 0))
    return pl.pallas_call(
        _split_kernel,
        out_shape=(jax.ShapeDtypeStruct((depth, d, N_MIX), BF16),
                   jax.ShapeDtypeStruct((depth, d, FG_PAD), BF16),
                   jax.ShapeDtypeStruct((depth, d, n_gate), BF16)),
        grid=(depth, d // rows),
        in_specs=[pl.BlockSpec((None, rows, cols), lambda l, i: (l, i, 0))],
        out_specs=(out(N_MIX), out(FG_PAD), out(n_gate)),
        compiler_params=_params("parallel", "parallel"),
        name="split_w_in",
    )(w_in)


def _addnorm_kernel(*refs, has_delta, emit_stream):
    h_ref, refs = refs[0], refs[1:]
    h = h_ref[...]
    if has_delta:
        h = h + jnp.transpose(refs[0][...])
        refs = refs[1:]
    g_ref, refs = refs[0], refs[1:]
    if emit_stream:
        refs[0][...] = h
        refs = refs[1:]
    refs[0][...] = _rmsnorm_rows(h, g_ref[...]).astype(refs[0].dtype)


def _addnorm(h, delta_t, g, out_dtype, tm, want_stream):
    s, d = h.shape
    has_delta = delta_t is not None
    row_spec = pl.BlockSpec((tm, d), lambda i: (i, 0))
    in_specs = [row_spec] + ([pl.BlockSpec((d, tm), lambda i: (0, i))] if has_delta else []) + [
        pl.BlockSpec((1, d), lambda i: (0, 0))]
    out_shape = [jax.ShapeDtypeStruct((s, d), out_dtype)]
    out_specs = [row_spec]
    emit_stream = has_delta and want_stream
    if emit_stream:
        out_shape.insert(0, jax.ShapeDtypeStruct((s, d), F32))
        out_specs.insert(0, row_spec)
    outs = pl.pallas_call(
        functools.partial(_addnorm_kernel, has_delta=has_delta, emit_stream=emit_stream),
        out_shape=tuple(out_shape),
        grid=(s // tm,),
        in_specs=in_specs,
        out_specs=tuple(out_specs),
        compiler_params=_params("parallel"),
        name="addnorm",
    )(*((h, delta_t) if has_delta else (h,)), g)
    return outs if emit_stream else (h, outs[0])


def _inproj_kernel(xn_ref, w_ref, wfg_ref, z_ref, fg_ref):
    @pl.when(pl.program_id(1) == 0)
    def _():
        fg_ref[...] = jnp.dot(xn_ref[...], wfg_ref[...], preferred_element_type=F32)

    z_ref[...] = jnp.dot(xn_ref[...], w_ref[...], preferred_element_type=F32)


def _inproj(xn, w_mix, w_fg, l, tm, tn):
    s, d = xn.shape
    n = w_mix.shape[2]
    return pl.pallas_call(
        _inproj_kernel,
        out_shape=(jax.ShapeDtypeStruct((s, n), F32), jax.ShapeDtypeStruct((s, FG_PAD), F32)),
        grid=(s // tm, n // tn),
        in_specs=[pl.BlockSpec((tm, d), lambda i, j: (i, 0)),
                  pl.BlockSpec((None, d, tn), lambda i, j: (l, 0, j)),
                  pl.BlockSpec((None, d, FG_PAD), lambda i, j: (l, 0, 0))],
        out_specs=(pl.BlockSpec((tm, tn), lambda i, j: (i, j)),
                   pl.BlockSpec((tm, FG_PAD), lambda i, j: (i, 0))),
        compiler_params=_params("parallel", "arbitrary"),
        name="inproj",
    )(xn, w_mix, w_fg)


def _gates_kernel(xn_ref, w_ref, o_ref):
    z = jnp.dot(xn_ref[...], w_ref[...], preferred_element_type=F32)
    o_ref[...] = (1.0 / (1.0 + jnp.exp(-z))).astype(o_ref.dtype)


def _gates(xn, w_gate, l, tm, tn):
    s, d = xn.shape
    n = w_gate.shape[2]
    return pl.pallas_call(
        _gates_kernel,
        out_shape=jax.ShapeDtypeStruct((s, n), BF16),
        grid=(s // tm, n // tn),
        in_specs=[pl.BlockSpec((tm, d), lambda i, j: (i, 0)),
                  pl.BlockSpec((None, d, tn), lambda i, j: (l, 0, j))],
        out_specs=pl.BlockSpec((tm, tn), lambda i, j: (i, j)),
        compiler_params=_params("parallel", "arbitrary"),
        name="gates",
    )(xn, w_gate)


CUM_BLOCK = 256


def _fcum_kernel(x_ref, b_ref, tri_ref, o_ref):
    s = x_ref.shape[1]
    carry = jnp.zeros((x_ref.shape[0], 1), F32)
    for c in range(s // CUM_BLOCK):
        sl = slice(c * CUM_BLOCK, (c + 1) * CUM_BLOCK)
        x = x_ref[:, sl] + b_ref[...]
        ls = jnp.minimum(x, 0.0) - jnp.log1p(jnp.exp(-jnp.abs(x)))
        y = jnp.dot(ls, tri_ref[...], precision=lax.Precision.HIGHEST,
                    preferred_element_type=F32) + carry
        o_ref[:, sl] = y
        carry = y[:, CUM_BLOCK - 1:CUM_BLOCK]


def _fcum(f_t, b_col):
    tri = jnp.asarray(np.triu(np.ones((CUM_BLOCK, CUM_BLOCK), np.float32)))
    return pl.pallas_call(
        _fcum_kernel,
        out_shape=jax.ShapeDtypeStruct(f_t.shape, F32),
        compiler_params=pltpu.CompilerParams(vmem_limit_bytes=V7X_VMEM_LIMIT_BYTES),
        name="fcum",
    )(f_t, b_col, tri)


def _attn_kernel(qi_ref, kj_ref, q_ref, k_ref, v_ref, fq_ref, fk_ref, o_ref, m_ref, l_ref, acc_ref,
                 *, tq, tk, scale):
    p = pl.program_id(0)
    qi = qi_ref[p]
    kj = kj_ref[p]

    @pl.when(kj == 0)
    def _():
        m_ref[...] = jnp.full(m_ref.shape, NEG_BIG, F32)
        l_ref[...] = jnp.zeros(l_ref.shape, F32)
        acc_ref[...] = jnp.zeros(acc_ref.shape, F32)

    def step(masked):
        if masked:
            rows = qi * tq + lax.broadcasted_iota(jnp.int32, (tq, tk), 0)
            cols = kj * tk + lax.broadcasted_iota(jnp.int32, (tq, tk), 1)
            keep = rows >= cols
        ones = jnp.ones((tk, HEAD_DIM), BF16)
        for h in range(GROUPS):
            cs = slice(h * HEAD_DIM, (h + 1) * HEAD_DIM)
            q = q_ref[:, cs].astype(BF16)
            k = k_ref[:, cs].astype(BF16)
            s = lax.dot_general(q, k, (((1,), (1,)), ((), ())), preferred_element_type=F32) * (scale * LOG2E)
            fq = fq_ref[h]
            s = s + (fq[:, 0:1] - fk_ref[h]) * LOG2E
            if masked:
                s = jnp.where(keep, s, NEG_BIG)
            m_prev = m_ref[h]
            m_new = jnp.maximum(m_prev, jnp.max(s, axis=-1, keepdims=True))
            alpha = jnp.exp2(m_prev - m_new)
            pr = jnp.exp2(s - jnp.concatenate([m_new] * (tk // HEAD_DIM), axis=1)).astype(BF16)
            pv = jnp.dot(pr, jnp.concatenate([v_ref[:, cs].astype(BF16), ones], axis=1),
                         preferred_element_type=F32)
            l_ref[h] = alpha * l_ref[h] + pv[:, HEAD_DIM:]
            acc_ref[:, cs] = alpha * acc_ref[:, cs] + pv[:, :HEAD_DIM]
            m_ref[h] = m_new

    last = (qi * tq + tq - 1) // tk
    first_masked = (qi * tq) // tk

    @pl.when(kj < first_masked)
    def _():
        step(False)

    @pl.when(kj >= first_masked)
    def _():
        step(True)

    @pl.when(kj == last)
    def _():
        for h in range(GROUPS):
            cs = slice(h * HEAD_DIM, (h + 1) * HEAD_DIM)
            o_ref[:, cs] = (acc_ref[:, cs] / l_ref[h]).astype(o_ref.dtype)


def _attention(zmix, f_rows, tq, tk):
    s = zmix.shape[0]
    nq = s // tq
    pairs = [(i, j) for i in range(nq) for j in range((i * tq + tq - 1) // tk + 1)]
    qi = jnp.asarray(np.array([p[0] for p in pairs], np.int32))
    kj = jnp.asarray(np.array([p[1] for p in pairs], np.int32))
    qcol = OFF_D // BRANCH_W
    grid_spec = pltpu.PrefetchScalarGridSpec(
        num_scalar_prefetch=2,
        grid=(len(pairs),),
        in_specs=[
            pl.BlockSpec((tq, BRANCH_W), lambda p, qi, kj: (qi[p], qcol)),
            pl.BlockSpec((tk, BRANCH_W), lambda p, qi, kj: (kj[p], qcol + 1)),
            pl.BlockSpec((tk, BRANCH_W), lambda p, qi, kj: (kj[p], qcol + 2)),
            pl.BlockSpec((GROUPS, 1, tq), lambda p, qi, kj: (0, 0, qi[p])),
            pl.BlockSpec((GROUPS, 1, tk), lambda p, qi, kj: (0, 0, kj[p])),
        ],
        out_specs=pl.BlockSpec((tq, BRANCH_W), lambda p, qi, kj: (qi[p], 0)),
        scratch_shapes=[pltpu.VMEM((GROUPS, tq, HEAD_DIM), F32), pltpu.VMEM((GROUPS, tq, HEAD_DIM), F32),
                        pltpu.VMEM((tq, BRANCH_W), F32)],
    )
    return pl.pallas_call(
        functools.partial(_attn_kernel, tq=tq, tk=tk, scale=HEAD_DIM ** -0.5),
        out_shape=jax.ShapeDtypeStruct((s, BRANCH_W), BF16),
        grid_spec=grid_spec,
        compiler_params=_params("arbitrary"),
        name="fox_attention",
    )(qi, kj, zmix, zmix, zmix, f_rows, f_rows)


def _mix_kernel(z_ref, zp_ref, cw_ref, ng_ref, ws_ref, sb_ref, pw_ref, ps_ref, o_ref, ext_ref, *, tb):
    i = pl.program_id(0)
    bw = BRANCH_W

    prev = zp_ref[...]
    prev = jnp.where(i > 0, prev, jnp.zeros_like(prev))
    ext_ref[0:HIST, 0:bw] = prev[:, bw:2 * bw] * prev[:, 2 * bw:3 * bw]
    ext_ref[HIST:HIST + tb, 0:bw] = z_ref[:, bw:2 * bw] * z_ref[:, 2 * bw:3 * bw]
    ext_ref[0:HIST, bw:2 * bw] = prev[:, OFF_C:OFF_C + bw]
    ext_ref[HIST:HIST + tb, bw:2 * bw] = z_ref[:, OFF_C:OFF_C + bw]
    conv = cw_ref[CONV_W - 1:CONV_W, :] * ext_ref[HIST:HIST + tb, 0:bw]
    for lag in range(1, CONV_W):
        conv = conv + cw_ref[CONV_W - 1 - lag:CONV_W - lag, :] * ext_ref[HIST - lag:HIST - lag + tb, 0:bw]
    o_ref[:, 0:bw] = (z_ref[:, 0:bw] * conv).astype(o_ref.dtype)

    u = _gelu_tanh(z_ref[:, OFF_B:OFF_B + bw])
    v = _gelu_tanh(z_ref[:, OFF_B + bw:OFF_B + 2 * bw])
    v = _rmsnorm_rows(v, ng_ref[...]).astype(BF16)
    for c in range(tb // CHUNK):
        rs = slice(c * CHUNK, (c + 1) * CHUNK)
        for g in range(GROUPS):
            cs = slice(g * HEAD_DIM, (g + 1) * HEAD_DIM)
            sv = jnp.dot(ws_ref[g], v[rs, cs], preferred_element_type=F32) + sb_ref[g]
            o_ref[rs, bw + g * HEAD_DIM:bw + (g + 1) * HEAD_DIM] = (u[rs, cs] * sv).astype(o_ref.dtype)

    t = (i * tb + 1 + lax.broadcasted_iota(jnp.int32, (tb, 1), 0)).astype(F32)
    for g, w in enumerate(POOL_WINDOWS):
        cs = slice(bw + g * HEAD_DIM, bw + (g + 1) * HEAD_DIM)
        cur = ext_ref[HIST:HIST + tb, cs]
        tot = cur
        for lag in range(1, w):
            tot = tot + ext_ref[HIST - lag:HIST - lag + tb, cs]
        pooled = (tot / jnp.minimum(t, float(w)) - cur).astype(BF16)
        y = jnp.dot(pooled, pw_ref[g], preferred_element_type=F32)
        y = y * ps_ref[:, g * HEAD_DIM:(g + 1) * HEAD_DIM]
        o_ref[:, 2 * bw + g * HEAD_DIM:2 * bw + (g + 1) * HEAD_DIM] = y.astype(o_ref.dtype)


def _mixers(zmix, conv_w, sgu_norm_g, sgu_w_tril, sgu_bias_b, pool_w, pool_scale, tb):
    s = zmix.shape[0]
    ncol = OFF_D
    hist_blocks = tb // HIST
    const2 = lambda i: (0, 0)
    const3 = lambda i: (0, 0, 0)
    return pl.pallas_call(
        functools.partial(_mix_kernel, tb=tb),
        out_shape=jax.ShapeDtypeStruct((s, 3 * BRANCH_W), BF16),
        grid=(s // tb,),
        in_specs=[pl.BlockSpec((tb, ncol), lambda i: (i, 0)),
                  pl.BlockSpec((HIST, ncol), lambda i: (jnp.maximum(i * hist_blocks - 1, 0), 0)),
                  pl.BlockSpec((CONV_W, BRANCH_W), const2),
                  pl.BlockSpec((1, BRANCH_W), const2),
                  pl.BlockSpec((GROUPS, CHUNK, CHUNK), const3),
                  pl.BlockSpec((GROUPS, CHUNK, HEAD_DIM), const3),
                  pl.BlockSpec((GROUPS, HEAD_DIM, HEAD_DIM), const3),
                  pl.BlockSpec((1, BRANCH_W), const2)],
        out_specs=pl.BlockSpec((tb, 3 * BRANCH_W), lambda i: (i, 0)),
        scratch_shapes=[pltpu.VMEM((HIST + tb, 2 * BRANCH_W), F32)],
        compiler_params=_params("parallel"),
        name="mixers",
    )(zmix, zmix, conv_w, sgu_norm_g, sgu_w_tril, sgu_bias_b, pool_w, pool_scale)


def _merge_kernel(abc_ref, od_ref, gates_ref, wb_ref, o_ref):
    d = o_ref.shape[1]
    acc = None
    for n in range(N_BRANCH):
        br = od_ref[...] if n == N_BRANCH - 1 else abc_ref[:, n * BRANCH_W:(n + 1) * BRANCH_W]
        y = jnp.dot(br, wb_ref[n], preferred_element_type=F32)
        term = gates_ref[:, n * d:(n + 1) * d].astype(F32) * y
        acc = term if acc is None else acc + term
    o_ref[...] = acc.astype(o_ref.dtype)


def _merge(abc, od, gates, w_branch, l, tm):
    s = abc.shape[0]
    d = w_branch.shape[3]
    return pl.pallas_call(
        _merge_kernel,
        out_shape=jax.ShapeDtypeStruct((s, d), BF16),
        grid=(s // tm,),
        in_specs=[pl.BlockSpec((tm, 3 * BRANCH_W), lambda i: (i, 0)),
                  pl.BlockSpec((tm, BRANCH_W), lambda i: (i, 0)),
                  pl.BlockSpec((tm, N_BRANCH * d), lambda i: (i, 0)),
                  _resident((None, N_BRANCH, BRANCH_W, d), lambda i: (l, 0, 0, 0))],
        out_specs=pl.BlockSpec((tm, d), lambda i: (i, 0)),
        compiler_params=_params("parallel"),
        name="merge",
    )(abc, od, gates, w_branch)


def _outproj_kernel(m_ref, w_ref, h_ref, g_ref, h1_ref, hn_ref, hnt_ref):
    h1 = h_ref[...] + jnp.dot(m_ref[...], w_ref[...], preferred_element_type=F32)
    h1_ref[...] = h1
    hn = _rmsnorm_rows(h1, g_ref[...])
    hn_ref[...] = hn.astype(hn_ref.dtype)
    hnt_ref[...] = jnp.transpose(hn).astype(hnt_ref.dtype)


def _outproj(merged, w_out, l, h, g2, tm):
    s, d = h.shape
    return pl.pallas_call(
        _outproj_kernel,
        out_shape=(jax.ShapeDtypeStruct((s, d), F32), jax.ShapeDtypeStruct((s, d), BF16),
                   jax.ShapeDtypeStruct((d, s), BF16)),
        grid=(s // tm,),
        in_specs=[pl.BlockSpec((tm, d), lambda i: (i, 0)),
                  _resident((None, d, d), lambda i: (l, 0, 0)),
                  pl.BlockSpec((tm, d), lambda i: (i, 0)),
                  pl.BlockSpec((1, d), lambda i: (0, 0))],
        out_specs=(pl.BlockSpec((tm, d), lambda i: (i, 0)),
                   pl.BlockSpec((tm, d), lambda i: (i, 0)),
                   pl.BlockSpec((d, tm), lambda i: (0, i))),
        compiler_params=_params("parallel"),
        name="outproj",
    )(merged, w_out, h, g2)


def _candidate_positions(t):
    row8 = lax.broadcasted_iota(jnp.int32, (8, t), 0)
    row16 = lax.broadcasted_iota(jnp.int32, (PEER_TOPK, t), 0)
    return jnp.concatenate(
        [row16.astype(F32)]
        + [(row8 + 16 * p).astype(F32) for p in range(1, 8)]
        + [((row8 + 8) * 16).astype(F32)], axis=0)


def _pair(x, y, op):
    return jnp.concatenate(
        [op(x[0:1], y)] + [op(x[p:p + 1], y[0:8]) for p in range(1, 8)] + [op(x[8:16], y[0:1])], axis=0)


def _top16(s, exact, want_rank=True):
    t = s.shape[1]
    row16 = lax.broadcasted_iota(jnp.int32, (PEER_TOPK, t), 0)
    sub_iota = lax.broadcasted_iota(jnp.int32, s.shape, 0).astype(F32) if exact else None
    rank = jnp.full(s.shape, float(PEER_TOPK), F32) if want_rank else None
    vals = jnp.zeros((PEER_TOPK, t), F32)
    work = s
    for r in range(PEER_TOPK):
        m = jnp.max(work, axis=0, keepdims=True)
        hit = work == m
        if exact:
            first = jnp.min(jnp.where(hit, sub_iota, float(PEER_KEYS)), axis=0, keepdims=True)
            hit = sub_iota == first
        if want_rank:
            rank = jnp.where(hit, float(r), rank)
        work = jnp.where(hit, -jnp.inf, work)
        vals = jnp.where(row16 == r, m, vals)
    inside = rank < float(PEER_TOPK) if want_rank else s >= vals[PEER_TOPK - 1:PEER_TOPK]
    return rank, vals, jnp.sum(jnp.where(inside, 1.0, 0.0), axis=0, keepdims=True)


def _select_pairs(cand, exact):
    pos = _candidate_positions(cand.shape[1]) if exact else None
    sel = jnp.zeros(cand.shape, F32)
    work = cand
    for _ in range(PEER_TOPK):
        m = jnp.max(work, axis=0, keepdims=True)
        hit = work == m
        if exact:
            first = jnp.min(jnp.where(hit, pos, 1e9), axis=0, keepdims=True)
            hit = pos == first
        sel = jnp.where(hit, 1.0, sel)
        work = jnp.where(hit, -jnp.inf, work)
    return sel, jnp.sum(sel, axis=0, keepdims=True)


def _retrieve_head(sc0, sc1, exact):
    t = sc0.shape[1]
    rank0, a, n0 = _top16(sc0, exact, want_rank=exact)
    rank1, b, n1 = _top16(sc1, exact)
    sel, n2 = _select_pairs(_pair(a, b, jnp.add), exact)
    wgt = _pair(jnp.exp(a - a[0:1]), jnp.exp(b - b[0:1]), jnp.multiply)
    z = jnp.sum(sel * wgt, axis=0, keepdims=True)
    row8 = lax.broadcasted_iota(jnp.int32, (8, t), 0)
    cnt_lo = jnp.zeros((8, t), F32)
    cnt_lo = jnp.where(row8 == 0, jnp.sum(sel[0:16], axis=0, keepdims=True), cnt_lo)
    for p in range(1, 8):
        cnt_lo = jnp.where(row8 == p, jnp.sum(sel[8 + 8 * p:16 + 8 * p], axis=0, keepdims=True), cnt_lo)
    cnt = jnp.concatenate([cnt_lo, sel[72:80]], axis=0)
    count = jnp.zeros((PEER_KEYS, t), F32)
    for r in range(PEER_TOPK):
        is_r = rank0 == float(r) if exact else sc0 == a[r:r + 1]
        count = jnp.where(is_r, cnt[r:r + 1], count)
    k = float(PEER_TOPK)
    ok = jnp.where((n0 == k) & (n1 == k) & (n2 == k), 1.0, 0.0)
    return rank1, jnp.exp(sc1 - b[0:1]), count, jnp.exp(sc0 - a[0:1]) / z, ok


def _retrieve_kernel(hn_ref, wq_ref, keys_ref, r1_ref, e1_ref, q_ref, c_ref, qs_ref):
    def project(h):
        qh = jnp.dot(hn_ref[...], wq_ref[h], preferred_element_type=F32).astype(qs_ref.dtype)
        qs_ref[2 * h] = qh[:, :PEER_HALF]
        qs_ref[2 * h + 1] = qh[:, PEER_HALF:]

    project(0)

    def head(h, carry):
        def scores(half):
            return lax.dot_general(keys_ref[h, half], qs_ref[2 * h + half],
                                   (((1,), (1,)), ((), ())), preferred_element_type=F32)

        sc0, sc1 = scores(0), scores(1)
        project(jnp.minimum(h + 1, PEER_HEADS - 1))

        def emit(exact):
            rank1, gate1, count, gate0, ok = _retrieve_head(sc0, sc1, exact)
            r1_ref[h] = rank1.astype(r1_ref.dtype)
            e1_ref[h] = gate1.astype(e1_ref.dtype)
            q_ref[h] = count
            c_ref[h] = gate0
            return ok

        ok = emit(False)

        @pl.when(jnp.min(ok) < 0.5)
        def _():
            emit(True)

        return carry

    lax.fori_loop(0, PEER_HEADS, head, 0)


def _retrieve(hn, wq, keys, l, tt):
    s, d = hn.shape
    shape = jax.ShapeDtypeStruct((PEER_HEADS, PEER_KEYS, s), F32)
    shape_lo = jax.ShapeDtypeStruct((PEER_HEADS, PEER_KEYS, s), BF16)
    out_spec = pl.BlockSpec((PEER_HEADS, PEER_KEYS, tt), lambda i: (0, 0, i))
    return pl.pallas_call(
        _retrieve_kernel,
        out_shape=(shape_lo, shape_lo, shape, shape),
        grid=(s // tt,),
        in_specs=[pl.BlockSpec((tt, d), lambda i: (i, 0)),
                  _resident((None,) + wq.shape[1:], lambda i: (l, 0, 0, 0)),
                  _resident((None,) + keys.shape[1:], lambda i: (l, 0, 0, 0, 0))],
        out_specs=(out_spec, out_spec, out_spec, out_spec),
        scratch_shapes=[pltpu.VMEM((2 * PEER_HEADS, tt, PEER_HALF), BF16)],
        compiler_params=_params("parallel"),
        name="peer_retrieve",
    )(hn, wq, keys)


EXPERT_TILE = 4 * PEER_KEYS
TILE_ROWS = EXPERT_TILE // PEER_KEYS
TOKEN_PIECE = 256


def _gate_tile(act, r1_ref, e1_ref, q_ref, c_ref, row0, cols):
    lo = e1_ref.dtype
    zero = jnp.zeros((), lo)
    blocks = []
    for ii in range(TILE_ROWS):
        i = row0 + ii
        w = None
        for h in range(PEER_HEADS):
            qrow = q_ref[h, i:i + 1, cols].astype(lo)
            crow = c_ref[h, i:i + 1, cols].astype(lo)
            gate = jnp.where(r1_ref[h, :, cols] < qrow, e1_ref[h, :, cols] * crow, zero)
            w = gate if w is None else w + gate
        blocks.append(w.astype(F32) * _gelu_tanh(act[ii * PEER_KEYS:(ii + 1) * PEER_KEYS]))
    return jnp.concatenate(blocks, axis=0)


def _experts_kernel(hnt_ref, u_ref, vt_ref, r1_ref, e1_ref, qa_ref, ca_ref, qb_ref, cb_ref, o_ref,
                    act_ref, a_ref):
    g = pl.program_id(1)
    last = pl.num_programs(1) - 1
    even, odd = slice(0, EXPERT_TILE), slice(EXPERT_TILE, 2 * EXPERT_TILE)

    @pl.when(g == 0)
    def _():
        o_ref[...] = jnp.zeros(o_ref.shape, o_ref.dtype)
        a_ref[0] = jnp.zeros(a_ref.shape[1:], a_ref.dtype)
        act_ref[1] = jnp.zeros(act_ref.shape[1:], act_ref.dtype)

    tt = hnt_ref.shape[1]
    pieces = [slice(c, c + TOKEN_PIECE) for c in range(0, tt, TOKEN_PIECE)]

    def first_matmul(slot, rows, tc):
        act_ref[slot, :, tc] = jnp.dot(u_ref[rows], hnt_ref[:, tc], preferred_element_type=F32)

    def gates(slot, q_ref, c_ref, row0, tc):
        a_ref[slot, :, tc] = _gate_tile(act_ref[slot, :, tc], r1_ref, e1_ref, q_ref, c_ref, row0,
                                        tc).astype(a_ref.dtype)

    def second_matmul(slot, cols, tc):
        o_ref[:, tc] += jnp.dot(vt_ref[:, cols], a_ref[slot, :, tc], preferred_element_type=F32)

    @pl.when(g < last)
    def _():
        for tc in pieces:
            second_matmul(0, even, tc)
            gates(1, qa_ref, ca_ref, TILE_ROWS, tc)
            first_matmul(0, even, tc)
        for tc in pieces:
            second_matmul(1, odd, tc)
            gates(0, qb_ref, cb_ref, 0, tc)
            first_matmul(1, odd, tc)

    @pl.when(g == last)
    def _():
        for tc in pieces:
            second_matmul(0, even, tc)
            gates(1, qa_ref, ca_ref, TILE_ROWS, tc)
        for tc in pieces:
            second_matmul(1, odd, tc)


def _experts(hn_t, u_tab, v_tab_t, l, r1, e1, q, c, tt):
    d, s = hn_t.shape
    n_exp = u_tab.shape[1]
    eb = 2 * EXPERT_TILE
    n_blocks = n_exp // eb
    behind = lambda e: jnp.maximum(e - 1, 0)
    ahead = lambda e: jnp.minimum(e, n_blocks - 1)
    sel_spec = pl.BlockSpec((PEER_HEADS, PEER_KEYS, tt), lambda i, e: (0, 0, i), pipeline_mode=pl.Buffered(1))
    row_block = (PEER_HEADS, eb // PEER_KEYS, tt)
    return pl.pallas_call(
        _experts_kernel,
        out_shape=jax.ShapeDtypeStruct((d, s), F32),
        grid=(s // tt, n_blocks + 1),
        in_specs=[pl.BlockSpec((d, tt), lambda i, e: (0, i), pipeline_mode=pl.Buffered(1)),
                  pl.BlockSpec((None, eb, d), lambda i, e: (l, ahead(e), 0)),
                  pl.BlockSpec((None, d, eb), lambda i, e: (l, 0, behind(e))),
                  sel_spec, sel_spec,
                  pl.BlockSpec(row_block, lambda i, e: (0, behind(e), i)),
                  pl.BlockSpec(row_block, lambda i, e: (0, behind(e), i)),
                  pl.BlockSpec(row_block, lambda i, e: (0, ahead(e), i)),
                  pl.BlockSpec(row_block, lambda i, e: (0, ahead(e), i))],
        out_specs=pl.BlockSpec((d, tt), lambda i, e: (0, i)),
        scratch_shapes=[pltpu.VMEM((2, EXPERT_TILE, tt), F32), pltpu.VMEM((2, EXPERT_TILE, tt), BF16)],
        compiler_params=_params("parallel", "arbitrary"),
        name="peer_experts",
    )(hn_t, u_tab, v_tab_t, r1, e1, q, c, q, c)


def _tile(s, want):
    t = min(s, want)
    assert s % t == 0
    return t


def _layer(h, delta_t, l, w, p):
    s, d = h.shape
    h, xn = _addnorm(h, delta_t, p["norm1_g"], BF16, _tile(s, 512), True)
    zmix, fg = _inproj(xn, w["w_mix"], w["w_fg"], l, _tile(s, 1024), 1536)
    gates = _gates(xn, w["w_gate"], l, _tile(s, 1024), 1024)

    f_t = jnp.transpose(fg[:, :8])
    f_cum = _fcum(f_t, p["forget_b"])
    f_rows = f_cum[:GROUPS].reshape(GROUPS, 1, s)
    od = _attention(zmix, f_rows, _tile(s, 512), _tile(s, 512))

    abc = _mixers(zmix, p["conv_w"], p["sgu_norm_g"], p["sgu_w"], p["sgu_b"], p["pool_w"],
                  p["pool_scale"], _tile(s, 512))
    merged = _merge(abc, od, gates, w["w_branch"], l, _tile(s, 512))
    h1, hn, hn_t = _outproj(merged, w["w_out"], l, h, p["norm2_g"], _tile(s, 512))

    r1, e1, q, c = _retrieve(hn, w["peer_wq"], w["peer_keys"], l, _tile(s, 512))
    return h1, _experts(hn_t, w["peer_u"], w["peer_v_t"], l, r1, e1, q, c, _tile(s, 1024))


def kernel(x, norm1_g, w_in, conv_w, sgu_norm_g, sgu_w, sgu_b, pool_w, pool_scale, forget_b,
           w_branch, w_out, norm2_g, peer_wq, peer_keys, peer_u, peer_v, final_g):
    bsz, s, d = x.shape
    depth = w_in.shape[0]
    off_g = N_MIX + GROUPS
    assert w_in.shape[2] == off_g + N_BRANCH * d
    tril = jnp.tril(jnp.ones((CHUNK, CHUNK), dtype=bool))

    w_mix, w_fg, w_gate = _split_w_in(w_in, 256)
    w = {
        "w_mix": w_mix,
        "w_fg": w_fg,
        "w_gate": w_gate,
        "w_branch": w_branch.astype(BF16),
        "w_out": w_out.astype(BF16),
        "peer_wq": jnp.transpose(
            peer_wq.astype(BF16).reshape(depth, d, PEER_HEADS, 2 * PEER_HALF), (0, 2, 1, 3)),
        "peer_keys": peer_keys.astype(BF16),
        "peer_u": peer_u.astype(BF16),
        "peer_v_t": jnp.transpose(peer_v.astype(BF16), (0, 2, 1)),
    }

    outs = []
    for b in range(bsz):
        h, delta = x[b], None
        for l in range(depth):
            p = {
                "norm1_g": norm1_g[l][None, :],
                "conv_w": conv_w[l],
                "sgu_norm_g": sgu_norm_g[l][None, :],
                "sgu_w": jnp.where(tril[None], sgu_w[l], 0.0).astype(BF16),
                "sgu_b": jnp.broadcast_to(sgu_b[l][:, :, None], (GROUPS, CHUNK, HEAD_DIM)),
                "pool_w": pool_w[l].astype(BF16),
                "pool_scale": pool_scale[l][None, :],
                "forget_b": jnp.pad(forget_b[l], (0, 8 - GROUPS))[:, None],
                "norm2_g": norm2_g[l][None, :],
            }
            h, delta = _layer(h, delta, l, w, p)
        outs.append(_addnorm(h, delta, final_g[None, :], F32, _tile(s, 512), False)[1])
    return jnp.stack(outs, axis=0)
```

```python
import functools
import math

import numpy as np
import jax
import jax.numpy as jnp
from jax import lax
from jax.experimental import pallas as pl
from jax.experimental.pallas import tpu as pltpu

F32 = jnp.float32
BF16 = jnp.bfloat16

HEAD_DIM = 128
GROUPS = 4
BRANCH_W = GROUPS * HEAD_DIM
N_BRANCH = 4
CONV_W = 3
CHUNK = 128
POOL_WINDOWS = (2, 4, 8, 16)
PEER_HEADS = 8
PEER_KEYS = 128
PEER_HALF = 128
PEER_TOPK = 16
EPS = 1e-6

OFF_B = 3 * BRANCH_W
OFF_C = OFF_B + 2 * BRANCH_W
OFF_D = OFF_C + BRANCH_W
N_MIX = OFF_D + 3 * BRANCH_W
HIST = 16
FG_PAD = 128

V7X_VMEM_LIMIT_BYTES = 56 * 1024 * 1024
NEG_BIG = -1e30
LOG2E = math.log2(math.e)


def _params(*sem, flags=None):
    return pltpu.CompilerParams(dimension_semantics=sem, vmem_limit_bytes=V7X_VMEM_LIMIT_BYTES, flags=flags)


def _gelu_tanh(x):
    return 0.5 * x * (1.0 + jnp.tanh(math.sqrt(2.0 / math.pi) * (x + 0.044715 * (x * x * x))))


def _rmsnorm_rows(x, g):
    return x * lax.rsqrt(jnp.mean(x * x, axis=-1, keepdims=True) + EPS) * g


def _resident(shape, index_map):
    return pl.BlockSpec(shape, index_map, pipeline_mode=pl.Buffered(1))


NT_DIMS = (((1,), (1,)), ((), ()))


def _addnorm_kernel(*refs, has_delta, emit_stream):
    h_ref, refs = refs[0], refs[1:]
    h = h_ref[...]
    if has_delta:
        h = h + jnp.transpose(refs[0][...])
        refs = refs[1:]
    g_ref, refs = refs[0], refs[1:]
    if emit_stream:
        refs[0][...] = h
        refs = refs[1:]
    refs[0][...] = _rmsnorm_rows(h, g_ref[...]).astype(refs[0].dtype)


def _addnorm(h, delta_t, g, out_dtype, tm, want_stream):
    s, d = h.shape
    has_delta = delta_t is not None
    row_spec = pl.BlockSpec((tm, d), lambda i: (i, 0))
    in_specs = [row_spec] + ([pl.BlockSpec((d, tm), lambda i: (0, i))] if has_delta else []) + [
        pl.BlockSpec((1, d), lambda i: (0, 0))]
    out_shape = [jax.ShapeDtypeStruct((s, d), out_dtype)]
    out_specs = [row_spec]
    emit_stream = has_delta and want_stream
    if emit_stream:
        out_shape.insert(0, jax.ShapeDtypeStruct((s, d), F32))
        out_specs.insert(0, row_spec)
    outs = pl.pallas_call(
        functools.partial(_addnorm_kernel, has_delta=has_delta, emit_stream=emit_stream),
        out_shape=tuple(out_shape),
        grid=(s // tm,),
        in_specs=in_specs,
        out_specs=tuple(out_specs),
        compiler_params=_params("parallel"),
        name="addnorm",
    )(*((h, delta_t) if has_delta else (h,)), g)
    return outs if emit_stream else (h, outs[0])


def _inproj_kernel(xn_ref, w_ref, wfg_ref, z_ref, fg_ref):
    @pl.when(pl.program_id(1) == 0)
    def _():
        fg_ref[...] = lax.dot_general(xn_ref[...], wfg_ref[...], NT_DIMS, preferred_element_type=F32)

    z_ref[...] = lax.dot_general(xn_ref[...], w_ref[...], NT_DIMS, preferred_element_type=F32)


def _inproj(xn, w_mix_t, w_fg_t, l, tm, tn):
    s, d = xn.shape
    n = w_mix_t.shape[0]
    return pl.pallas_call(
        _inproj_kernel,
        out_shape=(jax.ShapeDtypeStruct((s, n), F32), jax.ShapeDtypeStruct((s, FG_PAD), F32)),
        grid=(s // tm, n // tn),
        in_specs=[pl.BlockSpec((tm, d), lambda i, j: (i, 0)),
                  pl.BlockSpec((tn, d), lambda i, j: (j, l)),
                  pl.BlockSpec((FG_PAD, d), lambda i, j: (0, l))],
        out_specs=(pl.BlockSpec((tm, tn), lambda i, j: (i, j)),
                   pl.BlockSpec((tm, FG_PAD), lambda i, j: (i, 0))),
        compiler_params=_params("parallel", "arbitrary"),
        name="inproj",
    )(xn, w_mix_t, w_fg_t)


def _gates_kernel(xn_ref, w_ref, o_ref):
    z = lax.dot_general(xn_ref[...], w_ref[...], NT_DIMS, preferred_element_type=F32)
    o_ref[...] = (1.0 / (1.0 + jnp.exp(-z))).astype(o_ref.dtype)


def _gates(xn, w_gate_t, l, tm, tn):
    s, d = xn.shape
    n = w_gate_t.shape[0]
    return pl.pallas_call(
        _gates_kernel,
        out_shape=jax.ShapeDtypeStruct((s, n), BF16),
        grid=(s // tm, n // tn),
        in_specs=[pl.BlockSpec((tm, d), lambda i, j: (i, 0)),
                  pl.BlockSpec((tn, d), lambda i, j: (j, l))],
        out_specs=pl.BlockSpec((tm, tn), lambda i, j: (i, j)),
        compiler_params=_params("parallel", "arbitrary"),
        name="gates",
    )(xn, w_gate_t)


CUM_BLOCK = 256


def _fcum_kernel(x_ref, b_ref, tri_ref, o_ref):
    s = x_ref.shape[1]
    carry = jnp.zeros((x_ref.shape[0], 1), F32)
    for c in range(s // CUM_BLOCK):
        sl = slice(c * CUM_BLOCK, (c + 1) * CUM_BLOCK)
        x = x_ref[:, sl] + b_ref[...]
        ls = jnp.minimum(x, 0.0) - jnp.log1p(jnp.exp(-jnp.abs(x)))
        y = jnp.dot(ls, tri_ref[...], precision=lax.Precision.HIGHEST,
                    preferred_element_type=F32) + carry
        o_ref[:, sl] = y
        carry = y[:, CUM_BLOCK - 1:CUM_BLOCK]


def _fcum(f_t, b_col):
    tri = jnp.asarray(np.triu(np.ones((CUM_BLOCK, CUM_BLOCK), np.float32)))
    return pl.pallas_call(
        _fcum_kernel,
        out_shape=jax.ShapeDtypeStruct(f_t.shape, F32),
        compiler_params=pltpu.CompilerParams(vmem_limit_bytes=V7X_VMEM_LIMIT_BYTES),
        name="fcum",
    )(f_t, b_col, tri)


def _attn_kernel(qi_ref, kj_ref, q_ref, k_ref, v_ref, fq_ref, fk_ref, o_ref, m_ref, l_ref, acc_ref,
                 *, tq, tk, scale):
    p = pl.program_id(0)
    qi = qi_ref[p]
    kj = kj_ref[p]

    @pl.when(kj == 0)
    def _():
        m_ref[...] = jnp.full(m_ref.shape, NEG_BIG, F32)
        l_ref[...] = jnp.zeros(l_ref.shape, F32)
        acc_ref[...] = jnp.zeros(acc_ref.shape, F32)

    def step(masked):
        if masked:
            rows = qi * tq + lax.broadcasted_iota(jnp.int32, (tq, tk), 0)
            cols = kj * tk + lax.broadcasted_iota(jnp.int32, (tq, tk), 1)
            keep = rows >= cols
        ones = jnp.ones((tk, HEAD_DIM), BF16)
        for h in range(GROUPS):
            cs = slice(h * HEAD_DIM, (h + 1) * HEAD_DIM)
            q = q_ref[:, cs].astype(BF16)
            k = k_ref[:, cs].astype(BF16)
            s = lax.dot_general(q, k, (((1,), (1,)), ((), ())), preferred_element_type=F32) * (scale * LOG2E)
            fq = fq_ref[h]
            s = s + (fq[:, 0:1] - fk_ref[h]) * LOG2E
            if masked:
                s = jnp.where(keep, s, NEG_BIG)
            m_prev = m_ref[h]
            m_new = jnp.maximum(m_prev, jnp.max(s, axis=-1, keepdims=True))
            alpha = jnp.exp2(m_prev - m_new)
            pr = jnp.exp2(s - jnp.concatenate([m_new] * (tk // HEAD_DIM), axis=1)).astype(BF16)
            pv = jnp.dot(pr, jnp.concatenate([v_ref[:, cs].astype(BF16), ones], axis=1),
                         preferred_element_type=F32)
            l_ref[h] = alpha * l_ref[h] + pv[:, HEAD_DIM:]
            acc_ref[:, cs] = alpha * acc_ref[:, cs] + pv[:, :HEAD_DIM]
            m_ref[h] = m_new

    last = (qi * tq + tq - 1) // tk
    first_masked = (qi * tq) // tk

    @pl.when(kj < first_masked)
    def _():
        step(False)

    @pl.when(kj >= first_masked)
    def _():
        step(True)

    @pl.when(kj == last)
    def _():
        for h in range(GROUPS):
            cs = slice(h * HEAD_DIM, (h + 1) * HEAD_DIM)
            o_ref[:, cs] = (acc_ref[:, cs] / l_ref[h]).astype(o_ref.dtype)


def _attention(zmix, f_rows, tq, tk):
    s = zmix.shape[0]
    nq = s // tq
    pairs = [(i, j) for i in range(nq) for j in range((i * tq + tq - 1) // tk + 1)]
    qi = jnp.asarray(np.array([p[0] for p in pairs], np.int32))
    kj = jnp.asarray(np.array([p[1] for p in pairs], np.int32))
    qcol = OFF_D // BRANCH_W
    grid_spec = pltpu.PrefetchScalarGridSpec(
        num_scalar_prefetch=2,
        grid=(len(pairs),),
        in_specs=[
            pl.BlockSpec((tq, BRANCH_W), lambda p, qi, kj: (qi[p], qcol)),
            pl.BlockSpec((tk, BRANCH_W), lambda p, qi, kj: (kj[p], qcol + 1)),
            pl.BlockSpec((tk, BRANCH_W), lambda p, qi, kj: (kj[p], qcol + 2)),
            pl.BlockSpec((GROUPS, 1, tq), lambda p, qi, kj: (0, 0, qi[p])),
            pl.BlockSpec((GROUPS, 1, tk), lambda p, qi, kj: (0, 0, kj[p])),
        ],
        out_specs=pl.BlockSpec((tq, BRANCH_W), lambda p, qi, kj: (qi[p], 0)),
        scratch_shapes=[pltpu.VMEM((GROUPS, tq, HEAD_DIM), F32), pltpu.VMEM((GROUPS, tq, HEAD_DIM), F32),
                        pltpu.VMEM((tq, BRANCH_W), F32)],
    )
    return pl.pallas_call(
        functools.partial(_attn_kernel, tq=tq, tk=tk, scale=HEAD_DIM ** -0.5),
        out_shape=jax.ShapeDtypeStruct((s, BRANCH_W), BF16),
        grid_spec=grid_spec,
        compiler_params=_params("arbitrary"),
        name="fox_attention",
    )(qi, kj, zmix, zmix, zmix, f_rows, f_rows)


def _mix_kernel(z_ref, zp_ref, cw_ref, ng_ref, ws_ref, sb_ref, pw_ref, ps_ref, o_ref, ext_ref, *, tb):
    i = pl.program_id(0)
    bw = BRANCH_W

    prev = zp_ref[...]
    prev = jnp.where(i > 0, prev, jnp.zeros_like(prev))
    ext_ref[0:HIST, 0:bw] = prev[:, bw:2 * bw] * prev[:, 2 * bw:3 * bw]
    ext_ref[HIST:HIST + tb, 0:bw] = z_ref[:, bw:2 * bw] * z_ref[:, 2 * bw:3 * bw]
    ext_ref[0:HIST, bw:2 * bw] = prev[:, OFF_C:OFF_C + bw]
    ext_ref[HIST:HIST + tb, bw:2 * bw] = z_ref[:, OFF_C:OFF_C + bw]
    conv = cw_ref[CONV_W - 1:CONV_W, :] * ext_ref[HIST:HIST + tb, 0:bw]
    for lag in range(1, CONV_W):
        conv = conv + cw_ref[CONV_W - 1 - lag:CONV_W - lag, :] * ext_ref[HIST - lag:HIST - lag + tb, 0:bw]
    o_ref[:, 0:bw] = (z_ref[:, 0:bw] * conv).astype(o_ref.dtype)

    u = _gelu_tanh(z_ref[:, OFF_B:OFF_B + bw])
    v = _gelu_tanh(z_ref[:, OFF_B + bw:OFF_B + 2 * bw])
    v = _rmsnorm_rows(v, ng_ref[...]).astype(BF16)
    for c in range(tb // CHUNK):
        rs = slice(c * CHUNK, (c + 1) * CHUNK)
        for g in range(GROUPS):
            cs = slice(g * HEAD_DIM, (g + 1) * HEAD_DIM)
            sv = jnp.dot(ws_ref[g], v[rs, cs], preferred_element_type=F32) + sb_ref[g]
            o_ref[rs, bw + g * HEAD_DIM:bw + (g + 1) * HEAD_DIM] = (u[rs, cs] * sv).astype(o_ref.dtype)

    t = (i * tb + 1 + lax.broadcasted_iota(jnp.int32, (tb, 1), 0)).astype(F32)
    for g, w in enumerate(POOL_WINDOWS):
        cs = slice(bw + g * HEAD_DIM, bw + (g + 1) * HEAD_DIM)
        cur = ext_ref[HIST:HIST + tb, cs]
        tot = cur
        for lag in range(1, w):
            tot = tot + ext_ref[HIST - lag:HIST - lag + tb, cs]
        pooled = (tot / jnp.minimum(t, float(w)) - cur).astype(BF16)
        y = jnp.dot(pooled, pw_ref[g], preferred_element_type=F32)
        y = y * ps_ref[:, g * HEAD_DIM:(g + 1) * HEAD_DIM]
        o_ref[:, 2 * bw + g * HEAD_DIM:2 * bw + (g + 1) * HEAD_DIM] = y.astype(o_ref.dtype)


def _mixers(zmix, conv_w, sgu_norm_g, sgu_w_tril, sgu_bias_b, pool_w, pool_scale, tb):
    s = zmix.shape[0]
    ncol = OFF_D
    hist_blocks = tb // HIST
    const2 = lambda i: (0, 0)
    const3 = lambda i: (0, 0, 0)
    return pl.pallas_call(
        functools.partial(_mix_kernel, tb=tb),
        out_shape=jax.ShapeDtypeStruct((s, 3 * BRANCH_W), BF16),
        grid=(s // tb,),
        in_specs=[pl.BlockSpec((tb, ncol), lambda i: (i, 0)),
                  pl.BlockSpec((HIST, ncol), lambda i: (jnp.maximum(i * hist_blocks - 1, 0), 0)),
                  pl.BlockSpec((CONV_W, BRANCH_W), const2),
                  pl.BlockSpec((1, BRANCH_W), const2),
                  pl.BlockSpec((GROUPS, CHUNK, CHUNK), const3),
                  pl.BlockSpec((GROUPS, CHUNK, HEAD_DIM), const3),
                  pl.BlockSpec((GROUPS, HEAD_DIM, HEAD_DIM), const3),
                  pl.BlockSpec((1, BRANCH_W), const2)],
        out_specs=pl.BlockSpec((tb, 3 * BRANCH_W), lambda i: (i, 0)),
        scratch_shapes=[pltpu.VMEM((HIST + tb, 2 * BRANCH_W), F32)],
        compiler_params=_params("parallel"),
        name="mixers",
    )(zmix, zmix, conv_w, sgu_norm_g, sgu_w_tril, sgu_bias_b, pool_w, pool_scale)


def _merge_kernel(abc_ref, od_ref, gates_ref, wb_ref, o_ref):
    d = o_ref.shape[1]
    acc = None
    for n in range(N_BRANCH):
        br = od_ref[...] if n == N_BRANCH - 1 else abc_ref[:, n * BRANCH_W:(n + 1) * BRANCH_W]
        y = jnp.dot(br, wb_ref[n], preferred_element_type=F32)
        term = gates_ref[:, n * d:(n + 1) * d].astype(F32) * y
        acc = term if acc is None else acc + term
    o_ref[...] = acc.astype(o_ref.dtype)


def _merge(abc, od, gates, w_branch, l, tm):
    s = abc.shape[0]
    d = w_branch.shape[3]
    return pl.pallas_call(
        _merge_kernel,
        out_shape=jax.ShapeDtypeStruct((s, d), BF16),
        grid=(s // tm,),
        in_specs=[pl.BlockSpec((tm, 3 * BRANCH_W), lambda i: (i, 0)),
                  pl.BlockSpec((tm, BRANCH_W), lambda i: (i, 0)),
                  pl.BlockSpec((tm, N_BRANCH * d), lambda i: (i, 0)),
                  _resident((None, N_BRANCH, BRANCH_W, d), lambda i: (l, 0, 0, 0))],
        out_specs=pl.BlockSpec((tm, d), lambda i: (i, 0)),
        compiler_params=_params("parallel"),
        name="merge",
    )(abc, od, gates, w_branch)


def _outproj_kernel(m_ref, w_ref, h_ref, g_ref, h1_ref, hn_ref, hnt_ref):
    h1 = h_ref[...] + jnp.dot(m_ref[...], w_ref[...], preferred_element_type=F32)
    h1_ref[...] = h1
    hn = _rmsnorm_rows(h1, g_ref[...])
    hn_ref[...] = hn.astype(hn_ref.dtype)
    hnt_ref[...] = jnp.transpose(hn).astype(hnt_ref.dtype)


def _outproj(merged, w_out, l, h, g2, tm):
    s, d = h.shape
    return pl.pallas_call(
        _outproj_kernel,
        out_shape=(jax.ShapeDtypeStruct((s, d), F32), jax.ShapeDtypeStruct((s, d), BF16),
                   jax.ShapeDtypeStruct((d, s), BF16)),
        grid=(s // tm,),
        in_specs=[pl.BlockSpec((tm, d), lambda i: (i, 0)),
                  _resident((None, d, d), lambda i: (l, 0, 0)),
                  pl.BlockSpec((tm, d), lambda i: (i, 0)),
                  pl.BlockSpec((1, d), lambda i: (0, 0))],
        out_specs=(pl.BlockSpec((tm, d), lambda i: (i, 0)),
                   pl.BlockSpec((tm, d), lambda i: (i, 0)),
                   pl.BlockSpec((d, tm), lambda i: (0, i))),
        compiler_params=_params("parallel"),
        name="outproj",
    )(merged, w_out, h, g2)


def _candidate_positions(t):
    row8 = lax.broadcasted_iota(jnp.int32, (8, t), 0)
    row16 = lax.broadcasted_iota(jnp.int32, (PEER_TOPK, t), 0)
    return jnp.concatenate(
        [row16.astype(F32)]
        + [(row8 + 16 * p).astype(F32) for p in range(1, 8)]
        + [((row8 + 8) * 16).astype(F32)], axis=0)


def _pair(x, y, op):
    return jnp.concatenate(
        [op(x[0:1], y)] + [op(x[p:p + 1], y[0:8]) for p in range(1, 8)] + [op(x[8:16], y[0:1])], axis=0)


def _top16(s, exact, want_rank=True):
    t = s.shape[1]
    row16 = lax.broadcasted_iota(jnp.int32, (PEER_TOPK, t), 0)
    sub_iota = lax.broadcasted_iota(jnp.int32, s.shape, 0).astype(F32) if exact else None
    rank = jnp.full(s.shape, float(PEER_TOPK), F32) if want_rank else None
    vals = jnp.zeros((PEER_TOPK, t), F32)
    work = s
    for r in range(PEER_TOPK):
        m = jnp.max(work, axis=0, keepdims=True)
        hit = work == m
        if exact:
            first = jnp.min(jnp.where(hit, sub_iota, float(PEER_KEYS)), axis=0, keepdims=True)
            hit = sub_iota == first
        if want_rank:
            rank = jnp.where(hit, float(r), rank)
        work = jnp.where(hit, -jnp.inf, work)
        vals = jnp.where(row16 == r, m, vals)
    inside = rank < float(PEER_TOPK) if want_rank else s >= vals[PEER_TOPK - 1:PEER_TOPK]
    return rank, vals, jnp.sum(jnp.where(inside, 1.0, 0.0), axis=0, keepdims=True)


def _select_pairs(cand, exact):
    pos = _candidate_positions(cand.shape[1]) if exact else None
    sel = jnp.zeros(cand.shape, F32)
    work = cand
    for _ in range(PEER_TOPK):
        m = jnp.max(work, axis=0, keepdims=True)
        hit = work == m
        if exact:
            first = jnp.min(jnp.where(hit, pos, 1e9), axis=0, keepdims=True)
            hit = pos == first
        sel = jnp.where(hit, 1.0, sel)
        work = jnp.where(hit, -jnp.inf, work)
    return sel, jnp.sum(sel, axis=0, keepdims=True)


def _retrieve_head(sc0, sc1, exact):
    t = sc0.shape[1]
    rank0, a, n0 = _top16(sc0, exact, want_rank=exact)
    rank1, b, n1 = _top16(sc1, exact)
    sel, n2 = _select_pairs(_pair(a, b, jnp.add), exact)
    wgt = _pair(jnp.exp(a - a[0:1]), jnp.exp(b - b[0:1]), jnp.multiply)
    z = jnp.sum(sel * wgt, axis=0, keepdims=True)
    row8 = lax.broadcasted_iota(jnp.int32, (8, t), 0)
    cnt_lo = jnp.zeros((8, t), F32)
    cnt_lo = jnp.where(row8 == 0, jnp.sum(sel[0:16], axis=0, keepdims=True), cnt_lo)
    for p in range(1, 8):
        cnt_lo = jnp.where(row8 == p, jnp.sum(sel[8 + 8 * p:16 + 8 * p], axis=0, keepdims=True), cnt_lo)
    cnt = jnp.concatenate([cnt_lo, sel[72:80]], axis=0)
    count = jnp.zeros((PEER_KEYS, t), F32)
    for r in range(PEER_TOPK):
        is_r = rank0 == float(r) if exact else sc0 == a[r:r + 1]
        count = jnp.where(is_r, cnt[r:r + 1], count)
    k = float(PEER_TOPK)
    ok = jnp.where((n0 == k) & (n1 == k) & (n2 == k), 1.0, 0.0)
    return rank1, jnp.exp(sc1 - b[0:1]), count, jnp.exp(sc0 - a[0:1]) / z, ok


def _retrieve_kernel(hn_ref, wq_ref, keys_ref, r1_ref, e1_ref, q_ref, c_ref, qs_ref):
    def project(h):
        qh = jnp.dot(hn_ref[...], wq_ref[h], preferred_element_type=F32).astype(qs_ref.dtype)
        qs_ref[2 * h] = qh[:, :PEER_HALF]
        qs_ref[2 * h + 1] = qh[:, PEER_HALF:]

    project(0)

    def head(h, carry):
        def scores(half):
            return lax.dot_general(keys_ref[h, half], qs_ref[2 * h + half],
                                   (((1,), (1,)), ((), ())), preferred_element_type=F32)

        sc0, sc1 = scores(0), scores(1)
        project(jnp.minimum(h + 1, PEER_HEADS - 1))

        def emit(exact):
            rank1, gate1, count, gate0, ok = _retrieve_head(sc0, sc1, exact)
            r1_ref[h] = rank1.astype(r1_ref.dtype)
            e1_ref[h] = gate1.astype(e1_ref.dtype)
            q_ref[h] = count
            c_ref[h] = gate0
            return ok

        ok = emit(False)

        @pl.when(jnp.min(ok) < 0.5)
        def _():
            emit(True)

        return carry

    lax.fori_loop(0, PEER_HEADS, head, 0)


def _retrieve(hn, wq, keys, l, tt):
    s, d = hn.shape
    shape = jax.ShapeDtypeStruct((PEER_HEADS, PEER_KEYS, s), F32)
    shape_lo = jax.ShapeDtypeStruct((PEER_HEADS, PEER_KEYS, s), BF16)
    out_spec = pl.BlockSpec((PEER_HEADS, PEER_KEYS, tt), lambda i: (0, 0, i))
    return pl.pallas_call(
        _retrieve_kernel,
        out_shape=(shape_lo, shape_lo, shape, shape),
        grid=(s // tt,),
        in_specs=[pl.BlockSpec((tt, d), lambda i: (i, 0)),
                  _resident((None,) + wq.shape[1:], lambda i: (l, 0, 0, 0)),
                  _resident((None,) + keys.shape[1:], lambda i: (l, 0, 0, 0, 0))],
        out_specs=(out_spec, out_spec, out_spec, out_spec),
        scratch_shapes=[pltpu.VMEM((2 * PEER_HEADS, tt, PEER_HALF), BF16)],
        compiler_params=_params("parallel"),
        name="peer_retrieve",
    )(hn, wq, keys)


EXPERT_TILE = 4 * PEER_KEYS
TILE_ROWS = EXPERT_TILE // PEER_KEYS
TOKEN_PIECE = 256


def _gate_tile(act, r1_ref, e1_ref, q_ref, c_ref, row0, cols):
    lo = e1_ref.dtype
    zero = jnp.zeros((), lo)
    blocks = []
    for ii in range(TILE_ROWS):
        i = row0 + ii
        w = None
        for h in range(PEER_HEADS):
            qrow = q_ref[h, i:i + 1, cols].astype(lo)
            crow = c_ref[h, i:i + 1, cols].astype(lo)
            gate = jnp.where(r1_ref[h, :, cols] < qrow, e1_ref[h, :, cols] * crow, zero)
            w = gate if w is None else w + gate
        blocks.append(w.astype(F32) * _gelu_tanh(act[ii * PEER_KEYS:(ii + 1) * PEER_KEYS]))
    return jnp.concatenate(blocks, axis=0)


def _experts_kernel(hnt_ref, u_ref, vt_ref, r1_ref, e1_ref, qa_ref, ca_ref, qb_ref, cb_ref, o_ref,
                    act_ref, a_ref):
    g = pl.program_id(1)
    last = pl.num_programs(1) - 1
    even, odd = slice(0, EXPERT_TILE), slice(EXPERT_TILE, 2 * EXPERT_TILE)

    @pl.when(g == 0)
    def _():
        o_ref[...] = jnp.zeros(o_ref.shape, o_ref.dtype)
        a_ref[0] = jnp.zeros(a_ref.shape[1:], a_ref.dtype)
        act_ref[1] = jnp.zeros(act_ref.shape[1:], act_ref.dtype)

    tt = hnt_ref.shape[1]
    pieces = [slice(c, c + TOKEN_PIECE) for c in range(0, tt, TOKEN_PIECE)]

    def first_matmul(slot, rows, tc):
        act_ref[slot, :, tc] = jnp.dot(u_ref[rows], hnt_ref[:, tc], preferred_element_type=F32)

    def gates(slot, q_ref, c_ref, row0, tc):
        a_ref[slot, :, tc] = _gate_tile(act_ref[slot, :, tc], r1_ref, e1_ref, q_ref, c_ref, row0,
                                        tc).astype(a_ref.dtype)

    def second_matmul(slot, cols, tc):
        o_ref[:, tc] += jnp.dot(vt_ref[:, cols], a_ref[slot, :, tc], preferred_element_type=F32)

    @pl.when(g < last)
    def _():
        for tc in pieces:
            second_matmul(0, even, tc)
            gates(1, qa_ref, ca_ref, TILE_ROWS, tc)
            first_matmul(0, even, tc)
        for tc in pieces:
            second_matmul(1, odd, tc)
            gates(0, qb_ref, cb_ref, 0, tc)
            first_matmul(1, odd, tc)

    @pl.when(g == last)
    def _():
        for tc in pieces:
            second_matmul(0, even, tc)
            gates(1, qa_ref, ca_ref, TILE_ROWS, tc)
        for tc in pieces:
            second_matmul(1, odd, tc)


def _experts(hn_t, u_tab, v_tab_t, l, r1, e1, q, c, tt):
    d, s = hn_t.shape
    n_exp = u_tab.shape[1]
    eb = 2 * EXPERT_TILE
    n_blocks = n_exp // eb
    behind = lambda e: jnp.maximum(e - 1, 0)
    ahead = lambda e: jnp.minimum(e, n_blocks - 1)
    sel_spec = pl.BlockSpec((PEER_HEADS, PEER_KEYS, tt), lambda i, e: (0, 0, i), pipeline_mode=pl.Buffered(1))
    row_block = (PEER_HEADS, eb // PEER_KEYS, tt)
    return pl.pallas_call(
        _experts_kernel,
        out_shape=jax.ShapeDtypeStruct((d, s), F32),
        grid=(s // tt, n_blocks + 1),
        in_specs=[pl.BlockSpec((d, tt), lambda i, e: (0, i), pipeline_mode=pl.Buffered(1)),
                  pl.BlockSpec((None, eb, d), lambda i, e: (l, ahead(e), 0)),
                  pl.BlockSpec((None, d, eb), lambda i, e: (l, 0, behind(e))),
                  sel_spec, sel_spec,
                  pl.BlockSpec(row_block, lambda i, e: (0, behind(e), i)),
                  pl.BlockSpec(row_block, lambda i, e: (0, behind(e), i)),
                  pl.BlockSpec(row_block, lambda i, e: (0, ahead(e), i)),
                  pl.BlockSpec(row_block, lambda i, e: (0, ahead(e), i))],
        out_specs=pl.BlockSpec((d, tt), lambda i, e: (0, i)),
        scratch_shapes=[pltpu.VMEM((2, EXPERT_TILE, tt), F32), pltpu.VMEM((2, EXPERT_TILE, tt), BF16)],
        compiler_params=_params("parallel", "arbitrary"),
        name="peer_experts",
    )(hn_t, u_tab, v_tab_t, r1, e1, q, c, q, c)


def _tile(s, want):
    t = min(s, want)
    assert s % t == 0
    return t


def _layer(h, delta_t, l, w, p):
    s, d = h.shape
    h, xn = _addnorm(h, delta_t, p["norm1_g"], BF16, _tile(s, 512), True)
    zmix, fg = _inproj(xn, w["w_mix"], w["w_fg"], l, _tile(s, 1024), 1536)
    gates = _gates(xn, w["w_gate"], l, _tile(s, 1024), 1024)

    f_t = jnp.transpose(fg[:, :8])
    f_cum = _fcum(f_t, p["forget_b"])
    f_rows = f_cum[:GROUPS].reshape(GROUPS, 1, s)
    od = _attention(zmix, f_rows, _tile(s, 512), _tile(s, 512))

    abc = _mixers(zmix, p["conv_w"], p["sgu_norm_g"], p["sgu_w"], p["sgu_b"], p["pool_w"],
                  p["pool_scale"], _tile(s, 512))
    merged = _merge(abc, od, gates, w["w_branch"], l, _tile(s, 512))
    h1, hn, hn_t = _outproj(merged, w["w_out"], l, h, p["norm2_g"], _tile(s, 512))

    r1, e1, q, c = _retrieve(hn, w["peer_wq"], w["peer_keys"], l, _tile(s, 512))
    return h1, _experts(hn_t, w["peer_u"], w["peer_v_t"], l, r1, e1, q, c, _tile(s, 1024))


def kernel(x, norm1_g, w_in, conv_w, sgu_norm_g, sgu_w, sgu_b, pool_w, pool_scale, forget_b,
           w_branch, w_out, norm2_g, peer_wq, peer_keys, peer_u, peer_v, final_g):
    bsz, s, d = x.shape
    depth = w_in.shape[0]
    off_g = N_MIX + GROUPS
    assert w_in.shape[2] == off_g + N_BRANCH * d
    tril = jnp.tril(jnp.ones((CHUNK, CHUNK), dtype=bool))

    w_in_t = jnp.transpose(w_in, (2, 0, 1)).reshape(w_in.shape[2], depth * d)
    w = {
        "w_mix": w_in_t[:N_MIX].astype(BF16),
        "w_fg": jnp.pad(w_in_t[N_MIX:off_g], ((0, FG_PAD - GROUPS), (0, 0))).astype(BF16),
        "w_gate": w_in_t[off_g:].astype(BF16),
        "w_branch": w_branch.astype(BF16),
        "w_out": w_out.astype(BF16),
        "peer_wq": jnp.transpose(
            peer_wq.astype(BF16).reshape(depth, d, PEER_HEADS, 2 * PEER_HALF), (0, 2, 1, 3)),
        "peer_keys": peer_keys.astype(BF16),
        "peer_u": peer_u.astype(BF16),
        "peer_v_t": jnp.transpose(peer_v.astype(BF16), (0, 2, 1)),
    }

    outs = []
    for b in range(bsz):
        h, delta = x[b], None
        for l in range(depth):
            p = {
                "norm1_g": norm1_g[l][None, :],
                "conv_w": conv_w[l],
                "sgu_norm_g": sgu_norm_g[l][None, :],
                "sgu_w": jnp.where(tril[None], sgu_w[l], 0.0).astype(BF16),
                "sgu_b": jnp.broadcast_to(sgu_b[l][:, :, None], (GROUPS, CHUNK, HEAD_DIM)),
                "pool_w": pool_w[l].astype(BF16),
                "pool_scale": pool_scale[l][None, :],
                "forget_b": jnp.pad(forget_b[l], (0, 8 - GROUPS))[:, None],
                "norm2_g": norm2_g[l][None, :],
            }
            h, delta = _layer(h, delta, l, w, p)
        outs.append(_addnorm(h, delta, final_g[None, :], F32, _tile(s, 512), False)[1])
    return jnp.stack(outs, axis=0)
```

```python
import functools
import math

import numpy as np
import jax
import jax.numpy as jnp
from jax import lax
from jax.experimental import pallas as pl
from jax.experimental.pallas import tpu as pltpu

F32 = jnp.float32
BF16 = jnp.bfloat16

HEAD_DIM = 128
GROUPS = 4
BRANCH_W = GROUPS * HEAD_DIM
N_BRANCH = 4
CONV_W = 3
CHUNK = 128
POOL_WINDOWS = (2, 4, 8, 16)
PEER_HEADS = 8
PEER_KEYS = 128
PEER_HALF = 128
PEER_TOPK = 16
EPS = 1e-6

OFF_B = 3 * BRANCH_W
OFF_C = OFF_B + 2 * BRANCH_W
OFF_D = OFF_C + BRANCH_W
N_MIX = OFF_D + 3 * BRANCH_W
HIST = 16
FG_PAD = 128

V7X_VMEM_LIMIT_BYTES = 56 * 1024 * 1024
NEG_BIG = -1e30
LOG2E = math.log2(math.e)


def _params(*sem):
    return pltpu.CompilerParams(dimension_semantics=sem, vmem_limit_bytes=V7X_VMEM_LIMIT_BYTES)


def _gelu_tanh(x):
    return 0.5 * x * (1.0 + jnp.tanh(math.sqrt(2.0 / math.pi) * (x + 0.044715 * (x * x * x))))


def _rmsnorm_rows(x, g):
    return x * lax.rsqrt(jnp.mean(x * x, axis=-1, keepdims=True) + EPS) * g


def _resident(shape, index_map):
    return pl.BlockSpec(shape, index_map, pipeline_mode=pl.Buffered(1))


NT_DIMS = (((1,), (1,)), ((), ()))


def _addnorm_kernel(*refs, has_delta, emit_stream):
    h_ref, refs = refs[0], refs[1:]
    h = h_ref[...]
    if has_delta:
        h = h + jnp.transpose(refs[0][...])
        refs = refs[1:]
    g_ref, refs = refs[0], refs[1:]
    if emit_stream:
        refs[0][...] = h
        refs = refs[1:]
    refs[0][...] = _rmsnorm_rows(h, g_ref[...]).astype(refs[0].dtype)


def _addnorm(h, delta_t, g, out_dtype, tm, want_stream):
    s, d = h.shape
    has_delta = delta_t is not None
    row_spec = pl.BlockSpec((tm, d), lambda i: (i, 0))
    in_specs = [row_spec] + ([pl.BlockSpec((d, tm), lambda i: (0, i))] if has_delta else []) + [
        pl.BlockSpec((1, d), lambda i: (0, 0))]
    out_shape = [jax.ShapeDtypeStruct((s, d), out_dtype)]
    out_specs = [row_spec]
    emit_stream = has_delta and want_stream
    if emit_stream:
        out_shape.insert(0, jax.ShapeDtypeStruct((s, d), F32))
        out_specs.insert(0, row_spec)
    outs = pl.pallas_call(
        functools.partial(_addnorm_kernel, has_delta=has_delta, emit_stream=emit_stream),
        out_shape=tuple(out_shape),
        grid=(s // tm,),
        in_specs=in_specs,
        out_specs=tuple(out_specs),
        compiler_params=_params("parallel"),
        name="addnorm",
    )(*((h, delta_t) if has_delta else (h,)), g)
    return outs if emit_stream else (h, outs[0])


def _inproj_kernel(xn_ref, w_ref, wfg_ref, z_ref, fg_ref):
    @pl.when(pl.program_id(1) == 0)
    def _():
        fg_ref[...] = lax.dot_general(xn_ref[...], wfg_ref[...], NT_DIMS, preferred_element_type=F32)

    z_ref[...] = lax.dot_general(xn_ref[...], w_ref[...], NT_DIMS, preferred_element_type=F32)


def _inproj(xn, w_mix_t, w_fg_t, l, tm, tn):
    s, d = xn.shape
    n = w_mix_t.shape[0]
    return pl.pallas_call(
        _inproj_kernel,
        out_shape=(jax.ShapeDtypeStruct((s, n), F32), jax.ShapeDtypeStruct((s, FG_PAD), F32)),
        grid=(s // tm, n // tn),
        in_specs=[pl.BlockSpec((tm, d), lambda i, j: (i, 0)),
                  pl.BlockSpec((tn, d), lambda i, j: (j, l)),
                  pl.BlockSpec((FG_PAD, d), lambda i, j: (0, l))],
        out_specs=(pl.BlockSpec((tm, tn), lambda i, j: (i, j)),
                   pl.BlockSpec((tm, FG_PAD), lambda i, j: (i, 0))),
        compiler_params=_params("parallel", "arbitrary"),
        name="inproj",
    )(xn, w_mix_t, w_fg_t)


def _gates_kernel(xn_ref, w_ref, o_ref):
    z = lax.dot_general(xn_ref[...], w_ref[...], NT_DIMS, preferred_element_type=F32)
    o_ref[...] = (1.0 / (1.0 + jnp.exp(-z))).astype(o_ref.dtype)


def _gates(xn, w_gate_t, l, tm, tn):
    s, d = xn.shape
    n = w_gate_t.shape[0]
    return pl.pallas_call(
        _gates_kernel,
        out_shape=jax.ShapeDtypeStruct((s, n), BF16),
        grid=(s // tm, n // tn),
        in_specs=[pl.BlockSpec((tm, d), lambda i, j: (i, 0)),
                  pl.BlockSpec((tn, d), lambda i, j: (j, l))],
        out_specs=pl.BlockSpec((tm, tn), lambda i, j: (i, j)),
        compiler_params=_params("parallel", "arbitrary"),
        name="gates",
    )(xn, w_gate_t)


CUM_BLOCK = 256


def _fcum_kernel(x_ref, b_ref, tri_ref, o_ref):
    s = x_ref.shape[1]
    carry = jnp.zeros((x_ref.shape[0], 1), F32)
    for c in range(s // CUM_BLOCK):
        sl = slice(c * CUM_BLOCK, (c + 1) * CUM_BLOCK)
        x = x_ref[:, sl] + b_ref[...]
        ls = jnp.minimum(x, 0.0) - jnp.log1p(jnp.exp(-jnp.abs(x)))
        y = jnp.dot(ls, tri_ref[...], precision=lax.Precision.HIGHEST,
                    preferred_element_type=F32) + carry
        o_ref[:, sl] = y
        carry = y[:, CUM_BLOCK - 1:CUM_BLOCK]


def _fcum(f_t, b_col):
    tri = jnp.asarray(np.triu(np.ones((CUM_BLOCK, CUM_BLOCK), np.float32)))
    return pl.pallas_call(
        _fcum_kernel,
        out_shape=jax.ShapeDtypeStruct(f_t.shape, F32),
        compiler_params=pltpu.CompilerParams(vmem_limit_bytes=V7X_VMEM_LIMIT_BYTES),
        name="fcum",
    )(f_t, b_col, tri)


def _attn_kernel(qi_ref, kj_ref, q_ref, k_ref, v_ref, fq_ref, fk_ref, o_ref, m_ref, l_ref, acc_ref,
                 *, tq, tk, scale):
    p = pl.program_id(0)
    qi = qi_ref[p]
    kj = kj_ref[p]

    @pl.when(kj == 0)
    def _():
        m_ref[...] = jnp.full(m_ref.shape, NEG_BIG, F32)
        l_ref[...] = jnp.zeros(l_ref.shape, F32)
        acc_ref[...] = jnp.zeros(acc_ref.shape, F32)

    def step(masked):
        if masked:
            rows = qi * tq + lax.broadcasted_iota(jnp.int32, (tq, tk), 0)
            cols = kj * tk + lax.broadcasted_iota(jnp.int32, (tq, tk), 1)
            keep = rows >= cols
        ones = jnp.ones((tk, HEAD_DIM), BF16)
        for h in range(GROUPS):
            cs = slice(h * HEAD_DIM, (h + 1) * HEAD_DIM)
            q = q_ref[:, cs].astype(BF16)
            k = k_ref[:, cs].astype(BF16)
            s = lax.dot_general(q, k, (((1,), (1,)), ((), ())), preferred_element_type=F32) * (scale * LOG2E)
            fq = fq_ref[h]
            s = s + (fq[:, 0:1] - fk_ref[h]) * LOG2E
            if masked:
                s = jnp.where(keep, s, NEG_BIG)
            m_prev = m_ref[h]
            m_new = jnp.maximum(m_prev, jnp.max(s, axis=-1, keepdims=True))
            alpha = jnp.exp2(m_prev - m_new)
            pr = jnp.exp2(s - jnp.concatenate([m_new] * (tk // HEAD_DIM), axis=1)).astype(BF16)
            pv = jnp.dot(pr, jnp.concatenate([v_ref[:, cs].astype(BF16), ones], axis=1),
                         preferred_element_type=F32)
            l_ref[h] = alpha * l_ref[h] + pv[:, HEAD_DIM:]
            acc_ref[:, cs] = alpha * acc_ref[:, cs] + pv[:, :HEAD_DIM]
            m_ref[h] = m_new

    last = (qi * tq + tq - 1) // tk
    first_masked = (qi * tq) // tk

    @pl.when(kj < first_masked)
    def _():
        step(False)

    @pl.when(kj >= first_masked)
    def _():
        step(True)

    @pl.when(kj == last)
    def _():
        for h in range(GROUPS):
            cs = slice(h * HEAD_DIM, (h + 1) * HEAD_DIM)
            o_ref[:, cs] = (acc_ref[:, cs] / l_ref[h]).astype(o_ref.dtype)


def _attention(zmix, f_rows, tq, tk):
    s = zmix.shape[0]
    nq = s // tq
    pairs = [(i, j) for i in range(nq) for j in range((i * tq + tq - 1) // tk + 1)]
    qi = jnp.asarray(np.array([p[0] for p in pairs], np.int32))
    kj = jnp.asarray(np.array([p[1] for p in pairs], np.int32))
    qcol = OFF_D // BRANCH_W
    grid_spec = pltpu.PrefetchScalarGridSpec(
        num_scalar_prefetch=2,
        grid=(len(pairs),),
        in_specs=[
            pl.BlockSpec((tq, BRANCH_W), lambda p, qi, kj: (qi[p], qcol)),
            pl.BlockSpec((tk, BRANCH_W), lambda p, qi, kj: (kj[p], qcol + 1)),
            pl.BlockSpec((tk, BRANCH_W), lambda p, qi, kj: (kj[p], qcol + 2)),
            pl.BlockSpec((GROUPS, 1, tq), lambda p, qi, kj: (0, 0, qi[p])),
            pl.BlockSpec((GROUPS, 1, tk), lambda p, qi, kj: (0, 0, kj[p])),
        ],
        out_specs=pl.BlockSpec((tq, BRANCH_W), lambda p, qi, kj: (qi[p], 0)),
        scratch_shapes=[pltpu.VMEM((GROUPS, tq, HEAD_DIM), F32), pltpu.VMEM((GROUPS, tq, HEAD_DIM), F32),
                        pltpu.VMEM((tq, BRANCH_W), F32)],
    )
    return pl.pallas_call(
        functools.partial(_attn_kernel, tq=tq, tk=tk, scale=HEAD_DIM ** -0.5),
        out_shape=jax.ShapeDtypeStruct((s, BRANCH_W), BF16),
        grid_spec=grid_spec,
        compiler_params=_params("arbitrary"),
        name="fox_attention",
    )(qi, kj, zmix, zmix, zmix, f_rows, f_rows)


def _mix_kernel(z_ref, zp_ref, cw_ref, ng_ref, ws_ref, sb_ref, pw_ref, ps_ref, o_ref, ext_ref, *, tb):
    i = pl.program_id(0)
    bw = BRANCH_W

    prev = zp_ref[...]
    prev = jnp.where(i > 0, prev, jnp.zeros_like(prev))
    ext_ref[0:HIST, 0:bw] = prev[:, bw:2 * bw] * prev[:, 2 * bw:3 * bw]
    ext_ref[HIST:HIST + tb, 0:bw] = z_ref[:, bw:2 * bw] * z_ref[:, 2 * bw:3 * bw]
    ext_ref[0:HIST, bw:2 * bw] = prev[:, OFF_C:OFF_C + bw]
    ext_ref[HIST:HIST + tb, bw:2 * bw] = z_ref[:, OFF_C:OFF_C + bw]
    conv = cw_ref[CONV_W - 1:CONV_W, :] * ext_ref[HIST:HIST + tb, 0:bw]
    for lag in range(1, CONV_W):
        conv = conv + cw_ref[CONV_W - 1 - lag:CONV_W - lag, :] * ext_ref[HIST - lag:HIST - lag + tb, 0:bw]
    o_ref[:, 0:bw] = (z_ref[:, 0:bw] * conv).astype(o_ref.dtype)

    u = _gelu_tanh(z_ref[:, OFF_B:OFF_B + bw])
    v = _gelu_tanh(z_ref[:, OFF_B + bw:OFF_B + 2 * bw])
    v = _rmsnorm_rows(v, ng_ref[...]).astype(BF16)
    for c in range(tb // CHUNK):
        rs = slice(c * CHUNK, (c + 1) * CHUNK)
        for g in range(GROUPS):
            cs = slice(g * HEAD_DIM, (g + 1) * HEAD_DIM)
            sv = jnp.dot(ws_ref[g], v[rs, cs], preferred_element_type=F32) + sb_ref[g]
            o_ref[rs, bw + g * HEAD_DIM:bw + (g + 1) * HEAD_DIM] = (u[rs, cs] * sv).astype(o_ref.dtype)

    t = (i * tb + 1 + lax.broadcasted_iota(jnp.int32, (tb, 1), 0)).astype(F32)
    for g, w in enumerate(POOL_WINDOWS):
        cs = slice(bw + g * HEAD_DIM, bw + (g + 1) * HEAD_DIM)
        cur = ext_ref[HIST:HIST + tb, cs]
        tot = cur
        for lag in range(1, w):
            tot = tot + ext_ref[HIST - lag:HIST - lag + tb, cs]
        pooled = (tot / jnp.minimum(t, float(w)) - cur).astype(BF16)
        y = jnp.dot(pooled, pw_ref[g], preferred_element_type=F32)
        y = y * ps_ref[:, g * HEAD_DIM:(g + 1) * HEAD_DIM]
        o_ref[:, 2 * bw + g * HEAD_DIM:2 * bw + (g + 1) * HEAD_DIM] = y.astype(o_ref.dtype)


def _mixers(zmix, conv_w, sgu_norm_g, sgu_w_tril, sgu_bias_b, pool_w, pool_scale, tb):
    s = zmix.shape[0]
    ncol = OFF_D
    hist_blocks = tb // HIST
    const2 = lambda i: (0, 0)
    const3 = lambda i: (0, 0, 0)
    return pl.pallas_call(
        functools.partial(_mix_kernel, tb=tb),
        out_shape=jax.ShapeDtypeStruct((s, 3 * BRANCH_W), BF16),
        grid=(s // tb,),
        in_specs=[pl.BlockSpec((tb, ncol), lambda i: (i, 0)),
                  pl.BlockSpec((HIST, ncol), lambda i: (jnp.maximum(i * hist_blocks - 1, 0), 0)),
                  pl.BlockSpec((CONV_W, BRANCH_W), const2),
                  pl.BlockSpec((1, BRANCH_W), const2),
                  pl.BlockSpec((GROUPS, CHUNK, CHUNK), const3),
                  pl.BlockSpec((GROUPS, CHUNK, HEAD_DIM), const3),
                  pl.BlockSpec((GROUPS, HEAD_DIM, HEAD_DIM), const3),
                  pl.BlockSpec((1, BRANCH_W), const2)],
        out_specs=pl.BlockSpec((tb, 3 * BRANCH_W), lambda i: (i, 0)),
        scratch_shapes=[pltpu.VMEM((HIST + tb, 2 * BRANCH_W), F32)],
        compiler_params=_params("parallel"),
        name="mixers",
    )(zmix, zmix, conv_w, sgu_norm_g, sgu_w_tril, sgu_bias_b, pool_w, pool_scale)


def _merge_kernel(abc_ref, od_ref, gates_ref, wb_ref, o_ref):
    d = o_ref.shape[1]
    acc = None
    for n in range(N_BRANCH):
        br = od_ref[...] if n == N_BRANCH - 1 else abc_ref[:, n * BRANCH_W:(n + 1) * BRANCH_W]
        y = jnp.dot(br, wb_ref[n], preferred_element_type=F32)
        term = gates_ref[:, n * d:(n + 1) * d].astype(F32) * y
        acc = term if acc is None else acc + term
    o_ref[...] = acc.astype(o_ref.dtype)


def _merge(abc, od, gates, w_branch, l, tm):
    s = abc.shape[0]
    d = w_branch.shape[3]
    return pl.pallas_call(
        _merge_kernel,
        out_shape=jax.ShapeDtypeStruct((s, d), BF16),
        grid=(s // tm,),
        in_specs=[pl.BlockSpec((tm, 3 * BRANCH_W), lambda i: (i, 0)),
                  pl.BlockSpec((tm, BRANCH_W), lambda i: (i, 0)),
                  pl.BlockSpec((tm, N_BRANCH * d), lambda i: (i, 0)),
                  _resident((None, N_BRANCH, BRANCH_W, d), lambda i: (l, 0, 0, 0))],
        out_specs=pl.BlockSpec((tm, d), lambda i: (i, 0)),
        compiler_params=_params("parallel"),
        name="merge",
    )(abc, od, gates, w_branch)


def _outproj_kernel(m_ref, w_ref, h_ref, g_ref, h1_ref, hn_ref, hnt_ref):
    h1 = h_ref[...] + jnp.dot(m_ref[...], w_ref[...], preferred_element_type=F32)
    h1_ref[...] = h1
    hn = _rmsnorm_rows(h1, g_ref[...])
    hn_ref[...] = hn.astype(hn_ref.dtype)
    hnt_ref[...] = jnp.transpose(hn).astype(hnt_ref.dtype)


def _outproj(merged, w_out, l, h, g2, tm):
    s, d = h.shape
    return pl.pallas_call(
        _outproj_kernel,
        out_shape=(jax.ShapeDtypeStruct((s, d), F32), jax.ShapeDtypeStruct((s, d), BF16),
                   jax.ShapeDtypeStruct((d, s), BF16)),
        grid=(s // tm,),
        in_specs=[pl.BlockSpec((tm, d), lambda i: (i, 0)),
                  _resident((None, d, d), lambda i: (l, 0, 0)),
                  pl.BlockSpec((tm, d), lambda i: (i, 0)),
                  pl.BlockSpec((1, d), lambda i: (0, 0))],
        out_specs=(pl.BlockSpec((tm, d), lambda i: (i, 0)),
                   pl.BlockSpec((tm, d), lambda i: (i, 0)),
                   pl.BlockSpec((d, tm), lambda i: (0, i))),
        compiler_params=_params("parallel"),
        name="outproj",
    )(merged, w_out, h, g2)


def _candidate_positions(t):
    row8 = lax.broadcasted_iota(jnp.int32, (8, t), 0)
    row16 = lax.broadcasted_iota(jnp.int32, (PEER_TOPK, t), 0)
    return jnp.concatenate(
        [row16.astype(F32)]
        + [(row8 + 16 * p).astype(F32) for p in range(1, 8)]
        + [((row8 + 8) * 16).astype(F32)], axis=0)


def _pair(x, y, op):
    return jnp.concatenate(
        [op(x[0:1], y)] + [op(x[p:p + 1], y[0:8]) for p in range(1, 8)] + [op(x[8:16], y[0:1])], axis=0)


def _top16(s, exact, want_rank=True):
    t = s.shape[1]
    row16 = lax.broadcasted_iota(jnp.int32, (PEER_TOPK, t), 0)
    sub_iota = lax.broadcasted_iota(jnp.int32, s.shape, 0).astype(F32) if exact else None
    rank = jnp.full(s.shape, float(PEER_TOPK), F32) if want_rank else None
    vals = jnp.zeros((PEER_TOPK, t), F32)
    work = s
    for r in range(PEER_TOPK):
        m = jnp.max(work, axis=0, keepdims=True)
        hit = work == m
        if exact:
            first = jnp.min(jnp.where(hit, sub_iota, float(PEER_KEYS)), axis=0, keepdims=True)
            hit = sub_iota == first
        if want_rank:
            rank = jnp.where(hit, float(r), rank)
        work = jnp.where(hit, -jnp.inf, work)
        vals = jnp.where(row16 == r, m, vals)
    inside = rank < float(PEER_TOPK) if want_rank else s >= vals[PEER_TOPK - 1:PEER_TOPK]
    return rank, vals, jnp.sum(jnp.where(inside, 1.0, 0.0), axis=0, keepdims=True)


def _select_pairs(cand, exact):
    pos = _candidate_positions(cand.shape[1]) if exact else None
    sel = jnp.zeros(cand.shape, F32)
    work = cand
    for _ in range(PEER_TOPK):
        m = jnp.max(work, axis=0, keepdims=True)
        hit = work == m
        if exact:
            first = jnp.min(jnp.where(hit, pos, 1e9), axis=0, keepdims=True)
            hit = pos == first
        sel = jnp.where(hit, 1.0, sel)
        work = jnp.where(hit, -jnp.inf, work)
    return sel, jnp.sum(sel, axis=0, keepdims=True)


def _retrieve_head(sc0, sc1, exact):
    t = sc0.shape[1]
    rank0, a, n0 = _top16(sc0, exact, want_rank=exact)
    rank1, b, n1 = _top16(sc1, exact)
    sel, n2 = _select_pairs(_pair(a, b, jnp.add), exact)
    wgt = _pair(jnp.exp(a - a[0:1]), jnp.exp(b - b[0:1]), jnp.multiply)
    z = jnp.sum(sel * wgt, axis=0, keepdims=True)
    row8 = lax.broadcasted_iota(jnp.int32, (8, t), 0)
    cnt_lo = jnp.zeros((8, t), F32)
    cnt_lo = jnp.where(row8 == 0, jnp.sum(sel[0:16], axis=0, keepdims=True), cnt_lo)
    for p in range(1, 8):
        cnt_lo = jnp.where(row8 == p, jnp.sum(sel[8 + 8 * p:16 + 8 * p], axis=0, keepdims=True), cnt_lo)
    cnt = jnp.concatenate([cnt_lo, sel[72:80]], axis=0)
    count = jnp.zeros((PEER_KEYS, t), F32)
    for r in range(PEER_TOPK):
        is_r = rank0 == float(r) if exact else sc0 == a[r:r + 1]
        count = jnp.where(is_r, cnt[r:r + 1], count)
    k = float(PEER_TOPK)
    ok = jnp.where((n0 == k) & (n1 == k) & (n2 == k), 1.0, 0.0)
    return rank1, jnp.exp(sc1 - b[0:1]), count, jnp.exp(sc0 - a[0:1]) / z, ok


def _retrieve_kernel(hn_ref, wq_ref, keys_ref, r1_ref, e1_ref, q_ref, c_ref, qs_ref):
    def project(h):
        qh = jnp.dot(hn_ref[...], wq_ref[h], preferred_element_type=F32).astype(qs_ref.dtype)
        qs_ref[2 * h] = qh[:, :PEER_HALF]
        qs_ref[2 * h + 1] = qh[:, PEER_HALF:]

    project(0)

    def head(h, carry):
        def scores(half):
            return lax.dot_general(keys_ref[h, half], qs_ref[2 * h + half],
                                   (((1,), (1,)), ((), ())), preferred_element_type=F32)

        sc0, sc1 = scores(0), scores(1)
        project(jnp.minimum(h + 1, PEER_HEADS - 1))

        def emit(exact):
            rank1, gate1, count, gate0, ok = _retrieve_head(sc0, sc1, exact)
            r1_ref[h] = rank1.astype(r1_ref.dtype)
            e1_ref[h] = gate1.astype(e1_ref.dtype)
            q_ref[h] = count
            c_ref[h] = gate0
            return ok

        ok = emit(False)

        @pl.when(jnp.min(ok) < 0.5)
        def _():
            emit(True)

        return carry

    lax.fori_loop(0, PEER_HEADS, head, 0)


def _retrieve(hn, wq, keys, l, tt):
    s, d = hn.shape
    shape = jax.ShapeDtypeStruct((PEER_HEADS, PEER_KEYS, s), F32)
    shape_lo = jax.ShapeDtypeStruct((PEER_HEADS, PEER_KEYS, s), BF16)
    out_spec = pl.BlockSpec((PEER_HEADS, PEER_KEYS, tt), lambda i: (0, 0, i))
    return pl.pallas_call(
        _retrieve_kernel,
        out_shape=(shape_lo, shape_lo, shape, shape),
        grid=(s // tt,),
        in_specs=[pl.BlockSpec((tt, d), lambda i: (i, 0)),
                  _resident((None,) + wq.shape[1:], lambda i: (l, 0, 0, 0)),
                  _resident((None,) + keys.shape[1:], lambda i: (l, 0, 0, 0, 0))],
        out_specs=(out_spec, out_spec, out_spec, out_spec),
        scratch_shapes=[pltpu.VMEM((2 * PEER_HEADS, tt, PEER_HALF), BF16)],
        compiler_params=_params("parallel"),
        name="peer_retrieve",
    )(hn, wq, keys)


EXPERT_TILE = 4 * PEER_KEYS
TILE_ROWS = EXPERT_TILE // PEER_KEYS
TOKEN_PIECE = 256


def _gate_tile(act, r1_ref, e1_ref, q_ref, c_ref, row0, cols):
    lo = e1_ref.dtype
    zero = jnp.zeros((), lo)
    blocks = []
    for ii in range(TILE_ROWS):
        i = row0 + ii
        w = None
        for h in range(PEER_HEADS):
            qrow = q_ref[h, i:i + 1, cols].astype(lo)
            crow = c_ref[h, i:i + 1, cols].astype(lo)
            gate = jnp.where(r1_ref[h, :, cols] < qrow, e1_ref[h, :, cols] * crow, zero)
            w = gate if w is None else w + gate
        blocks.append(w.astype(F32) * _gelu_tanh(act[ii * PEER_KEYS:(ii + 1) * PEER_KEYS]))
    return jnp.concatenate(blocks, axis=0)


def _experts_kernel(hnt_ref, u_ref, vt_ref, r1_ref, e1_ref, qa_ref, ca_ref, qb_ref, cb_ref, o_ref,
                    act_ref, a_ref):
    g = pl.program_id(1)
    last = pl.num_programs(1) - 1
    even, odd = slice(0, EXPERT_TILE), slice(EXPERT_TILE, 2 * EXPERT_TILE)

    @pl.when(g == 0)
    def _():
        o_ref[...] = jnp.zeros(o_ref.shape, o_ref.dtype)
        a_ref[0] = jnp.zeros(a_ref.shape[1:], a_ref.dtype)
        act_ref[1] = jnp.zeros(act_ref.shape[1:], act_ref.dtype)

    tt = hnt_ref.shape[1]
    pieces = [slice(c, c + TOKEN_PIECE) for c in range(0, tt, TOKEN_PIECE)]

    def first_matmul(slot, rows, tc):
        act_ref[slot, :, tc] = jnp.dot(u_ref[rows], hnt_ref[:, tc], preferred_element_type=F32)

    def gates(slot, q_ref, c_ref, row0, tc):
        a_ref[slot, :, tc] = _gate_tile(act_ref[slot, :, tc], r1_ref, e1_ref, q_ref, c_ref, row0,
                                        tc).astype(a_ref.dtype)

    def second_matmul(slot, cols, tc):
        o_ref[:, tc] += jnp.dot(vt_ref[:, cols], a_ref[slot, :, tc], preferred_element_type=F32)

    @pl.when(g < last)
    def _():
        for tc in pieces:
            second_matmul(0, even, tc)
            gates(1, qa_ref, ca_ref, TILE_ROWS, tc)
            first_matmul(0, even, tc)
        for tc in pieces:
            second_matmul(1, odd, tc)
            gates(0, qb_ref, cb_ref, 0, tc)
            first_matmul(1, odd, tc)

    @pl.when(g == last)
    def _():
        for tc in pieces:
            second_matmul(0, even, tc)
            gates(1, qa_ref, ca_ref, TILE_ROWS, tc)
        for tc in pieces:
            second_matmul(1, odd, tc)


def _experts(hn_t, u_tab, v_tab_t, l, r1, e1, q, c, tt):
    d, s = hn_t.shape
    n_exp = u_tab.shape[1]
    eb = 2 * EXPERT_TILE
    n_blocks = n_exp // eb
    behind = lambda e: jnp.maximum(e - 1, 0)
    ahead = lambda e: jnp.minimum(e, n_blocks - 1)
    sel_spec = pl.BlockSpec((PEER_HEADS, PEER_KEYS, tt), lambda i, e: (0, 0, i), pipeline_mode=pl.Buffered(1))
    row_block = (PEER_HEADS, eb // PEER_KEYS, tt)
    return pl.pallas_call(
        _experts_kernel,
        out_shape=jax.ShapeDtypeStruct((d, s), F32),
        grid=(s // tt, n_blocks + 1),
        in_specs=[pl.BlockSpec((d, tt), lambda i, e: (0, i), pipeline_mode=pl.Buffered(1)),
                  pl.BlockSpec((None, eb, d), lambda i, e: (l, ahead(e), 0)),
                  pl.BlockSpec((None, d, eb), lambda i, e: (l, 0, behind(e))),
                  sel_spec, sel_spec,
                  pl.BlockSpec(row_block, lambda i, e: (0, behind(e), i)),
                  pl.BlockSpec(row_block, lambda i, e: (0, behind(e), i)),
                  pl.BlockSpec(row_block, lambda i, e: (0, ahead(e), i)),
                  pl.BlockSpec(row_block, lambda i, e: (0, ahead(e), i))],
        out_specs=pl.BlockSpec((d, tt), lambda i, e: (0, i)),
        scratch_shapes=[pltpu.VMEM((2, EXPERT_TILE, tt), F32), pltpu.VMEM((2, EXPERT_TILE, tt), BF16)],
        compiler_params=_params("parallel", "arbitrary"),
        name="peer_experts",
    )(hn_t, u_tab, v_tab_t, r1, e1, q, c, q, c)


TOKENS_ROWWISE = 512
TOKENS_MATMUL = 1024
COLS_INPROJ = 1536
COLS_GATES = 2048
TOKENS_ATTENTION = 512
TOKENS_RETRIEVE = 512
TOKENS_EXPERTS = 1024


def _tile(s, want):
    t = min(s, want)
    assert s % t == 0
    return t


def _layer(h, delta_t, l, w, p):
    s, d = h.shape
    rows = _tile(s, TOKENS_ROWWISE)
    h, xn = _addnorm(h, delta_t, p["norm1_g"], BF16, rows, True)
    zmix, fg = _inproj(xn, w["w_mix"], w["w_fg"], l, _tile(s, TOKENS_MATMUL), COLS_INPROJ)
    gates = _gates(xn, w["w_gate"], l, _tile(s, TOKENS_MATMUL), COLS_GATES)

    f_t = jnp.transpose(fg[:, :8])
    f_cum = _fcum(f_t, p["forget_b"])
    f_rows = f_cum[:GROUPS].reshape(GROUPS, 1, s)
    od = _attention(zmix, f_rows, _tile(s, TOKENS_ATTENTION), _tile(s, TOKENS_ATTENTION))

    abc = _mixers(zmix, p["conv_w"], p["sgu_norm_g"], p["sgu_w"], p["sgu_b"], p["pool_w"],
                  p["pool_scale"], rows)
    merged = _merge(abc, od, gates, w["w_branch"], l, rows)
    h1, hn, hn_t = _outproj(merged, w["w_out"], l, h, p["norm2_g"], rows)

    r1, e1, q, c = _retrieve(hn, w["peer_wq"], w["peer_keys"], l, _tile(s, TOKENS_RETRIEVE))
    return h1, _experts(hn_t, w["peer_u"], w["peer_v_t"], l, r1, e1, q, c, _tile(s, TOKENS_EXPERTS))


def kernel(x, norm1_g, w_in, conv_w, sgu_norm_g, sgu_w, sgu_b, pool_w, pool_scale, forget_b,
           w_branch, w_out, norm2_g, peer_wq, peer_keys, peer_u, peer_v, final_g):
    bsz, s, d = x.shape
    depth = w_in.shape[0]
    off_g = N_MIX + GROUPS
    assert w_in.shape[2] == off_g + N_BRANCH * d
    tril = jnp.tril(jnp.ones((CHUNK, CHUNK), dtype=bool))

    def columns_t(lo, hi):
        return jnp.transpose(w_in[:, :, lo:hi].astype(BF16), (2, 0, 1)).reshape(hi - lo, depth * d)

    w = {
        "w_mix": columns_t(0, N_MIX),
        "w_fg": jnp.pad(columns_t(N_MIX, off_g), ((0, FG_PAD - GROUPS), (0, 0))),
        "w_gate": columns_t(off_g, w_in.shape[2]),
        "w_branch": w_branch.astype(BF16),
        "w_out": w_out.astype(BF16),
        "peer_wq": jnp.transpose(
            peer_wq.astype(BF16).reshape(depth, d, PEER_HEADS, 2 * PEER_HALF), (0, 2, 1, 3)),
        "peer_keys": peer_keys.astype(BF16),
        "peer_u": peer_u.astype(BF16),
        "peer_v_t": jnp.transpose(peer_v.astype(BF16), (0, 2, 1)),
    }

    outs = []
    for b in range(bsz):
        h, delta = x[b], None
        for l in range(depth):
            p = {
                "norm1_g": norm1_g[l][None, :],
                "conv_w": conv_w[l],
                "sgu_norm_g": sgu_norm_g[l][None, :],
                "sgu_w": jnp.where(tril[None], sgu_w[l], 0.0).astype(BF16),
                "sgu_b": jnp.broadcast_to(sgu_b[l][:, :, None], (GROUPS, CHUNK, HEAD_DIM)),
                "pool_w": pool_w[l].astype(BF16),
                "pool_scale": pool_scale[l][None, :],
                "forget_b": jnp.pad(forget_b[l], (0, 8 - GROUPS))[:, None],
                "norm2_g": norm2_g[l][None, :],
            }
            h, delta = _layer(h, delta, l, w, p)
        outs.append(_addnorm(h, delta, final_g[None, :], F32, _tile(s, TOKENS_ROWWISE), False)[1])
    return jnp.stack(outs, axis=0)
```

```python
import functools
import math

import numpy as np
import jax
import jax.numpy as jnp
from jax import lax
from jax.experimental import pallas as pl
from jax.experimental.pallas import tpu as pltpu

F32 = jnp.float32
BF16 = jnp.bfloat16

HEAD_DIM = 128
GROUPS = 4
BRANCH_W = GROUPS * HEAD_DIM
N_BRANCH = 4
CONV_W = 3
CHUNK = 128
POOL_WINDOWS = (2, 4, 8, 16)
PEER_HEADS = 8
PEER_KEYS = 128
PEER_HALF = 128
PEER_TOPK = 16
EPS = 1e-6

OFF_B = 3 * BRANCH_W
OFF_C = OFF_B + 2 * BRANCH_W
OFF_D = OFF_C + BRANCH_W
N_MIX = OFF_D + 3 * BRANCH_W
HIST = 16
FG_PAD = 128

V7X_VMEM_LIMIT_BYTES = 56 * 1024 * 1024
NEG_BIG = -1e30
LOG2E = math.log2(math.e)


def _params(*sem):
    return pltpu.CompilerParams(dimension_semantics=sem, vmem_limit_bytes=V7X_VMEM_LIMIT_BYTES)


def _gelu_tanh(x):
    return 0.5 * x * (1.0 + jnp.tanh(math.sqrt(2.0 / math.pi) * (x + 0.044715 * (x * x * x))))


def _rmsnorm_rows(x, g):
    return x * lax.rsqrt(jnp.mean(x * x, axis=-1, keepdims=True) + EPS) * g


def _resident(shape, index_map):
    return pl.BlockSpec(shape, index_map, pipeline_mode=pl.Buffered(1))


NT_DIMS = (((1,), (1,)), ((), ()))


def _addnorm_kernel(*refs, has_delta, emit_stream):
    h_ref, refs = refs[0], refs[1:]
    h = h_ref[...]
    if has_delta:
        h = h + jnp.transpose(refs[0][...])
        refs = refs[1:]
    g_ref, refs = refs[0], refs[1:]
    if emit_stream:
        refs[0][...] = h
        refs = refs[1:]
    refs[0][...] = _rmsnorm_rows(h, g_ref[...]).astype(refs[0].dtype)


def _addnorm(h, delta_t, g, out_dtype, tm, want_stream):
    s, d = h.shape
    has_delta = delta_t is not None
    row_spec = pl.BlockSpec((tm, d), lambda i: (i, 0))
    in_specs = [row_spec] + ([pl.BlockSpec((d, tm), lambda i: (0, i))] if has_delta else []) + [
        pl.BlockSpec((1, d), lambda i: (0, 0))]
    out_shape = [jax.ShapeDtypeStruct((s, d), out_dtype)]
    out_specs = [row_spec]
    emit_stream = has_delta and want_stream
    if emit_stream:
        out_shape.insert(0, jax.ShapeDtypeStruct((s, d), F32))
        out_specs.insert(0, row_spec)
    outs = pl.pallas_call(
        functools.partial(_addnorm_kernel, has_delta=has_delta, emit_stream=emit_stream),
        out_shape=tuple(out_shape),
        grid=(s // tm,),
        in_specs=in_specs,
        out_specs=tuple(out_specs),
        compiler_params=_params("parallel"),
        name="addnorm",
    )(*((h, delta_t) if has_delta else (h,)), g)
    return outs if emit_stream else (h, outs[0])


def _inproj_kernel(xn_ref, w_ref, wfg_ref, z_ref, fg_ref):
    @pl.when(pl.program_id(1) == 0)
    def _():
        fg_ref[...] = lax.dot_general(xn_ref[...], wfg_ref[...], NT_DIMS, preferred_element_type=F32)

    z_ref[...] = lax.dot_general(xn_ref[...], w_ref[...], NT_DIMS, preferred_element_type=F32)


def _inproj(xn, w_mix_t, w_fg_t, l, tm, tn):
    s, d = xn.shape
    n = w_mix_t.shape[0]
    return pl.pallas_call(
        _inproj_kernel,
        out_shape=(jax.ShapeDtypeStruct((s, n), F32), jax.ShapeDtypeStruct((s, FG_PAD), F32)),
        grid=(s // tm, n // tn),
        in_specs=[pl.BlockSpec((tm, d), lambda i, j: (i, 0)),
                  pl.BlockSpec((tn, d), lambda i, j: (j, l)),
                  pl.BlockSpec((FG_PAD, d), lambda i, j: (0, l))],
        out_specs=(pl.BlockSpec((tm, tn), lambda i, j: (i, j)),
                   pl.BlockSpec((tm, FG_PAD), lambda i, j: (i, 0))),
        compiler_params=_params("parallel", "arbitrary"),
        name="inproj",
    )(xn, w_mix_t, w_fg_t)


def _gates_kernel(xn_ref, w_ref, o_ref):
    z = lax.dot_general(xn_ref[...], w_ref[...], NT_DIMS, preferred_element_type=F32)
    o_ref[...] = (1.0 / (1.0 + jnp.exp(-z))).astype(o_ref.dtype)


def _gates(xn, w_gate_t, l, tm, tn):
    s, d = xn.shape
    n = w_gate_t.shape[0]
    return pl.pallas_call(
        _gates_kernel,
        out_shape=jax.ShapeDtypeStruct((s, n), BF16),
        grid=(s // tm, n // tn),
        in_specs=[pl.BlockSpec((tm, d), lambda i, j: (i, 0)),
                  pl.BlockSpec((tn, d), lambda i, j: (j, l))],
        out_specs=pl.BlockSpec((tm, tn), lambda i, j: (i, j)),
        compiler_params=_params("parallel", "arbitrary"),
        name="gates",
    )(xn, w_gate_t)


CUM_BLOCK = 256


def _fcum_kernel(x_ref, b_ref, tri_ref, o_ref):
    s = x_ref.shape[1]
    carry = jnp.zeros((x_ref.shape[0], 1), F32)
    for c in range(s // CUM_BLOCK):
        sl = slice(c * CUM_BLOCK, (c + 1) * CUM_BLOCK)
        x = x_ref[:, sl] + b_ref[...]
        ls = jnp.minimum(x, 0.0) - jnp.log1p(jnp.exp(-jnp.abs(x)))
        y = jnp.dot(ls, tri_ref[...], precision=lax.Precision.HIGHEST,
                    preferred_element_type=F32) + carry
        o_ref[:, sl] = y
        carry = y[:, CUM_BLOCK - 1:CUM_BLOCK]


def _fcum(f_t, b_col):
    tri = jnp.asarray(np.triu(np.ones((CUM_BLOCK, CUM_BLOCK), np.float32)))
    return pl.pallas_call(
        _fcum_kernel,
        out_shape=jax.ShapeDtypeStruct(f_t.shape, F32),
        compiler_params=pltpu.CompilerParams(vmem_limit_bytes=V7X_VMEM_LIMIT_BYTES),
        name="fcum",
    )(f_t, b_col, tri)


def _attn_kernel(qi_ref, kj_ref, q_ref, k_ref, v_ref, fq_ref, fk_ref, o_ref, m_ref, l_ref, acc_ref,
                 *, tq, tk, scale):
    p = pl.program_id(0)
    qi = qi_ref[p]
    kj = kj_ref[p]

    @pl.when(kj == 0)
    def _():
        m_ref[...] = jnp.full(m_ref.shape, NEG_BIG, F32)
        l_ref[...] = jnp.zeros(l_ref.shape, F32)
        acc_ref[...] = jnp.zeros(acc_ref.shape, F32)

    def step(masked):
        if masked:
            rows = qi * tq + lax.broadcasted_iota(jnp.int32, (tq, tk), 0)
            cols = kj * tk + lax.broadcasted_iota(jnp.int32, (tq, tk), 1)
            keep = rows >= cols
        ones = jnp.ones((tk, HEAD_DIM), BF16)
        for h in range(GROUPS):
            cs = slice(h * HEAD_DIM, (h + 1) * HEAD_DIM)
            q = q_ref[:, cs].astype(BF16)
            k = k_ref[:, cs].astype(BF16)
            s = lax.dot_general(q, k, (((1,), (1,)), ((), ())), preferred_element_type=F32) * (scale * LOG2E)
            fq = fq_ref[h]
            s = s + (fq[:, 0:1] - fk_ref[h]) * LOG2E
            if masked:
                s = jnp.where(keep, s, NEG_BIG)
            m_prev = m_ref[h]
            m_new = jnp.maximum(m_prev, jnp.max(s, axis=-1, keepdims=True))
            alpha = jnp.exp2(m_prev - m_new)
            pr = jnp.exp2(s - jnp.concatenate([m_new] * (tk // HEAD_DIM), axis=1)).astype(BF16)
            pv = jnp.dot(pr, jnp.concatenate([v_ref[:, cs].astype(BF16), ones], axis=1),
                         preferred_element_type=F32)
            l_ref[h] = alpha * l_ref[h] + pv[:, HEAD_DIM:]
            acc_ref[:, cs] = alpha * acc_ref[:, cs] + pv[:, :HEAD_DIM]
            m_ref[h] = m_new

    last = (qi * tq + tq - 1) // tk
    first_masked = (qi * tq) // tk

    @pl.when(kj < first_masked)
    def _():
        step(False)

    @pl.when(kj >= first_masked)
    def _():
        step(True)

    @pl.when(kj == last)
    def _():
        for h in range(GROUPS):
            cs = slice(h * HEAD_DIM, (h + 1) * HEAD_DIM)
            o_ref[:, cs] = (acc_ref[:, cs] / l_ref[h]).astype(o_ref.dtype)


def _attention(zmix, f_rows, tq, tk):
    s = zmix.shape[0]
    nq = s // tq
    pairs = [(i, j) for i in range(nq) for j in range((i * tq + tq - 1) // tk + 1)]
    qi = jnp.asarray(np.array([p[0] for p in pairs], np.int32))
    kj = jnp.asarray(np.array([p[1] for p in pairs], np.int32))
    qcol = OFF_D // BRANCH_W
    grid_spec = pltpu.PrefetchScalarGridSpec(
        num_scalar_prefetch=2,
        grid=(len(pairs),),
        in_specs=[
            pl.BlockSpec((tq, BRANCH_W), lambda p, qi, kj: (qi[p], qcol)),
            pl.BlockSpec((tk, BRANCH_W), lambda p, qi, kj: (kj[p], qcol + 1)),
            pl.BlockSpec((tk, BRANCH_W), lambda p, qi, kj: (kj[p], qcol + 2)),
            pl.BlockSpec((GROUPS, 1, tq), lambda p, qi, kj: (0, 0, qi[p])),
            pl.BlockSpec((GROUPS, 1, tk), lambda p, qi, kj: (0, 0, kj[p])),
        ],
        out_specs=pl.BlockSpec((tq, BRANCH_W), lambda p, qi, kj: (qi[p], 0)),
        scratch_shapes=[pltpu.VMEM((GROUPS, tq, HEAD_DIM), F32), pltpu.VMEM((GROUPS, tq, HEAD_DIM), F32),
                        pltpu.VMEM((tq, BRANCH_W), F32)],
    )
    return pl.pallas_call(
        functools.partial(_attn_kernel, tq=tq, tk=tk, scale=HEAD_DIM ** -0.5),
        out_shape=jax.ShapeDtypeStruct((s, BRANCH_W), BF16),
        grid_spec=grid_spec,
        compiler_params=_params("arbitrary"),
        name="fox_attention",
    )(qi, kj, zmix, zmix, zmix, f_rows, f_rows)


def _mix_kernel(z_ref, zp_ref, cw_ref, ng_ref, ws_ref, sb_ref, pw_ref, ps_ref, o_ref, ext_ref, *, tb):
    i = pl.program_id(0)
    bw = BRANCH_W

    prev = zp_ref[...]
    prev = jnp.where(i > 0, prev, jnp.zeros_like(prev))
    ext_ref[0:HIST, 0:bw] = prev[:, bw:2 * bw] * prev[:, 2 * bw:3 * bw]
    ext_ref[HIST:HIST + tb, 0:bw] = z_ref[:, bw:2 * bw] * z_ref[:, 2 * bw:3 * bw]
    ext_ref[0:HIST, bw:2 * bw] = prev[:, OFF_C:OFF_C + bw]
    ext_ref[HIST:HIST + tb, bw:2 * bw] = z_ref[:, OFF_C:OFF_C + bw]
    conv = cw_ref[CONV_W - 1:CONV_W, :] * ext_ref[HIST:HIST + tb, 0:bw]
    for lag in range(1, CONV_W):
        conv = conv + cw_ref[CONV_W - 1 - lag:CONV_W - lag, :] * ext_ref[HIST - lag:HIST - lag + tb, 0:bw]
    o_ref[:, 0:bw] = (z_ref[:, 0:bw] * conv).astype(o_ref.dtype)

    u = _gelu_tanh(z_ref[:, OFF_B:OFF_B + bw])
    v = _gelu_tanh(z_ref[:, OFF_B + bw:OFF_B + 2 * bw])
    v = _rmsnorm_rows(v, ng_ref[...]).astype(BF16)
    for c in range(tb // CHUNK):
        rs = slice(c * CHUNK, (c + 1) * CHUNK)
        for g in range(GROUPS):
            cs = slice(g * HEAD_DIM, (g + 1) * HEAD_DIM)
            sv = jnp.dot(ws_ref[g], v[rs, cs], preferred_element_type=F32) + sb_ref[g]
            o_ref[rs, bw + g * HEAD_DIM:bw + (g + 1) * HEAD_DIM] = (u[rs, cs] * sv).astype(o_ref.dtype)

    t = (i * tb + 1 + lax.broadcasted_iota(jnp.int32, (tb, 1), 0)).astype(F32)
    for g, w in enumerate(POOL_WINDOWS):
        cs = slice(bw + g * HEAD_DIM, bw + (g + 1) * HEAD_DIM)
        cur = ext_ref[HIST:HIST + tb, cs]
        tot = cur
        for lag in range(1, w):
            tot = tot + ext_ref[HIST - lag:HIST - lag + tb, cs]
        pooled = (tot / jnp.minimum(t, float(w)) - cur).astype(BF16)
        y = jnp.dot(pooled, pw_ref[g], preferred_element_type=F32)
        y = y * ps_ref[:, g * HEAD_DIM:(g + 1) * HEAD_DIM]
        o_ref[:, 2 * bw + g * HEAD_DIM:2 * bw + (g + 1) * HEAD_DIM] = y.astype(o_ref.dtype)


def _mixers(zmix, conv_w, sgu_norm_g, sgu_w_tril, sgu_bias_b, pool_w, pool_scale, tb):
    s = zmix.shape[0]
    ncol = OFF_D
    hist_blocks = tb // HIST
    const2 = lambda i: (0, 0)
    const3 = lambda i: (0, 0, 0)
    return pl.pallas_call(
        functools.partial(_mix_kernel, tb=tb),
        out_shape=jax.ShapeDtypeStruct((s, 3 * BRANCH_W), BF16),
        grid=(s // tb,),
        in_specs=[pl.BlockSpec((tb, ncol), lambda i: (i, 0)),
                  pl.BlockSpec((HIST, ncol), lambda i: (jnp.maximum(i * hist_blocks - 1, 0), 0)),
                  pl.BlockSpec((CONV_W, BRANCH_W), const2),
                  pl.BlockSpec((1, BRANCH_W), const2),
                  pl.BlockSpec((GROUPS, CHUNK, CHUNK), const3),
                  pl.BlockSpec((GROUPS, CHUNK, HEAD_DIM), const3),
                  pl.BlockSpec((GROUPS, HEAD_DIM, HEAD_DIM), const3),
                  pl.BlockSpec((1, BRANCH_W), const2)],
        out_specs=pl.BlockSpec((tb, 3 * BRANCH_W), lambda i: (i, 0)),
        scratch_shapes=[pltpu.VMEM((HIST + tb, 2 * BRANCH_W), F32)],
        compiler_params=_params("parallel"),
        name="mixers",
    )(zmix, zmix, conv_w, sgu_norm_g, sgu_w_tril, sgu_bias_b, pool_w, pool_scale)


def _merge_kernel(abc_ref, od_ref, gates_ref, wb_ref, o_ref):
    d = o_ref.shape[1]
    acc = None
    for n in range(N_BRANCH):
        br = od_ref[...] if n == N_BRANCH - 1 else abc_ref[:, n * BRANCH_W:(n + 1) * BRANCH_W]
        y = jnp.dot(br, wb_ref[n], preferred_element_type=F32)
        term = gates_ref[:, n * d:(n + 1) * d].astype(F32) * y
        acc = term if acc is None else acc + term
    o_ref[...] = acc.astype(o_ref.dtype)


def _merge(abc, od, gates, w_branch, l, tm):
    s = abc.shape[0]
    d = w_branch.shape[3]
    return pl.pallas_call(
        _merge_kernel,
        out_shape=jax.ShapeDtypeStruct((s, d), BF16),
        grid=(s // tm,),
        in_specs=[pl.BlockSpec((tm, 3 * BRANCH_W), lambda i: (i, 0)),
                  pl.BlockSpec((tm, BRANCH_W), lambda i: (i, 0)),
                  pl.BlockSpec((tm, N_BRANCH * d), lambda i: (i, 0)),
                  _resident((None, N_BRANCH, BRANCH_W, d), lambda i: (l, 0, 0, 0))],
        out_specs=pl.BlockSpec((tm, d), lambda i: (i, 0)),
        compiler_params=_params("parallel"),
        name="merge",
    )(abc, od, gates, w_branch)


def _merge_outproj_kernel(abc_ref, od_ref, gates_ref, wb_ref, w_ref, h_ref, g_ref, h1_ref, hn_ref, hnt_ref):
    d = h_ref.shape[1]
    acc = None
    for n in range(N_BRANCH):
        br = od_ref[...] if n == N_BRANCH - 1 else abc_ref[:, n * BRANCH_W:(n + 1) * BRANCH_W]
        y = jnp.dot(br, wb_ref[n], preferred_element_type=F32)
        term = gates_ref[:, n * d:(n + 1) * d].astype(F32) * y
        acc = term if acc is None else acc + term
    h1 = h_ref[...] + jnp.dot(acc.astype(w_ref.dtype), w_ref[...], preferred_element_type=F32)
    h1_ref[...] = h1
    hn = _rmsnorm_rows(h1, g_ref[...])
    hn_ref[...] = hn.astype(hn_ref.dtype)
    hnt_ref[...] = jnp.transpose(hn).astype(hnt_ref.dtype)


def _merge_outproj(abc, od, gates, w_branch, w_out, l, h, g2, tm):
    s, d = h.shape
    row = lambda n: pl.BlockSpec((tm, n), lambda i: (i, 0))
    return pl.pallas_call(
        _merge_outproj_kernel,
        out_shape=(jax.ShapeDtypeStruct((s, d), F32), jax.ShapeDtypeStruct((s, d), BF16),
                   jax.ShapeDtypeStruct((d, s), BF16)),
        grid=(s // tm,),
        in_specs=[row(3 * BRANCH_W), row(BRANCH_W), row(N_BRANCH * d),
                  _resident((None, N_BRANCH, BRANCH_W, d), lambda i: (l, 0, 0, 0)),
                  _resident((None, d, d), lambda i: (l, 0, 0)),
                  row(d),
                  pl.BlockSpec((1, d), lambda i: (0, 0))],
        out_specs=(row(d), row(d), pl.BlockSpec((d, tm), lambda i: (0, i))),
        compiler_params=_params("parallel"),
        name="merge_outproj",
    )(abc, od, gates, w_branch, w_out, h, g2)


def _outproj_kernel(m_ref, w_ref, h_ref, g_ref, h1_ref, hn_ref, hnt_ref):
    h1 = h_ref[...] + jnp.dot(m_ref[...], w_ref[...], preferred_element_type=F32)
    h1_ref[...] = h1
    hn = _rmsnorm_rows(h1, g_ref[...])
    hn_ref[...] = hn.astype(hn_ref.dtype)
    hnt_ref[...] = jnp.transpose(hn).astype(hnt_ref.dtype)


def _outproj(merged, w_out, l, h, g2, tm):
    s, d = h.shape
    return pl.pallas_call(
        _outproj_kernel,
        out_shape=(jax.ShapeDtypeStruct((s, d), F32), jax.ShapeDtypeStruct((s, d), BF16),
                   jax.ShapeDtypeStruct((d, s), BF16)),
        grid=(s // tm,),
        in_specs=[pl.BlockSpec((tm, d), lambda i: (i, 0)),
                  _resident((None, d, d), lambda i: (l, 0, 0)),
                  pl.BlockSpec((tm, d), lambda i: (i, 0)),
                  pl.BlockSpec((1, d), lambda i: (0, 0))],
        out_specs=(pl.BlockSpec((tm, d), lambda i: (i, 0)),
                   pl.BlockSpec((tm, d), lambda i: (i, 0)),
                   pl.BlockSpec((d, tm), lambda i: (0, i))),
        compiler_params=_params("parallel"),
        name="outproj",
    )(merged, w_out, h, g2)


def _candidate_positions(t):
    row8 = lax.broadcasted_iota(jnp.int32, (8, t), 0)
    row16 = lax.broadcasted_iota(jnp.int32, (PEER_TOPK, t), 0)
    return jnp.concatenate(
        [row16.astype(F32)]
        + [(row8 + 16 * p).astype(F32) for p in range(1, 8)]
        + [((row8 + 8) * 16).astype(F32)], axis=0)


def _pair(x, y, op):
    return jnp.concatenate(
        [op(x[0:1], y)] + [op(x[p:p + 1], y[0:8]) for p in range(1, 8)] + [op(x[8:16], y[0:1])], axis=0)


def _top16(s, exact, want_rank=True):
    t = s.shape[1]
    row16 = lax.broadcasted_iota(jnp.int32, (PEER_TOPK, t), 0)
    sub_iota = lax.broadcasted_iota(jnp.int32, s.shape, 0).astype(F32) if exact else None
    rank = jnp.full(s.shape, float(PEER_TOPK), F32) if want_rank else None
    vals = jnp.zeros((PEER_TOPK, t), F32)
    work = s
    for r in range(PEER_TOPK):
        m = jnp.max(work, axis=0, keepdims=True)
        hit = work == m
        if exact:
            first = jnp.min(jnp.where(hit, sub_iota, float(PEER_KEYS)), axis=0, keepdims=True)
            hit = sub_iota == first
        if want_rank:
            rank = jnp.where(hit, float(r), rank)
        work = jnp.where(hit, -jnp.inf, work)
        vals = jnp.where(row16 == r, m, vals)
    inside = rank < float(PEER_TOPK) if want_rank else s >= vals[PEER_TOPK - 1:PEER_TOPK]
    return rank, vals, jnp.sum(jnp.where(inside, 1.0, 0.0), axis=0, keepdims=True)


def _select_pairs(cand, exact):
    pos = _candidate_positions(cand.shape[1]) if exact else None
    sel = jnp.zeros(cand.shape, F32)
    work = cand
    for _ in range(PEER_TOPK):
        m = jnp.max(work, axis=0, keepdims=True)
        hit = work == m
        if exact:
            first = jnp.min(jnp.where(hit, pos, 1e9), axis=0, keepdims=True)
            hit = pos == first
        sel = jnp.where(hit, 1.0, sel)
        work = jnp.where(hit, -jnp.inf, work)
    return sel, jnp.sum(sel, axis=0, keepdims=True)


def _retrieve_head(sc0, sc1, exact):
    t = sc0.shape[1]
    rank0, a, n0 = _top16(sc0, exact, want_rank=exact)
    rank1, b, n1 = _top16(sc1, exact)
    sel, n2 = _select_pairs(_pair(a, b, jnp.add), exact)
    wgt = _pair(jnp.exp(a - a[0:1]), jnp.exp(b - b[0:1]), jnp.multiply)
    z = jnp.sum(sel * wgt, axis=0, keepdims=True)
    row8 = lax.broadcasted_iota(jnp.int32, (8, t), 0)
    cnt_lo = jnp.zeros((8, t), F32)
    cnt_lo = jnp.where(row8 == 0, jnp.sum(sel[0:16], axis=0, keepdims=True), cnt_lo)
    for p in range(1, 8):
        cnt_lo = jnp.where(row8 == p, jnp.sum(sel[8 + 8 * p:16 + 8 * p], axis=0, keepdims=True), cnt_lo)
    cnt = jnp.concatenate([cnt_lo, sel[72:80]], axis=0)
    count = jnp.zeros((PEER_KEYS, t), F32)
    for r in range(PEER_TOPK):
        is_r = rank0 == float(r) if exact else sc0 == a[r:r + 1]
        count = jnp.where(is_r, cnt[r:r + 1], count)
    k = float(PEER_TOPK)
    ok = jnp.where((n0 == k) & (n1 == k) & (n2 == k), 1.0, 0.0)
    return rank1, jnp.exp(sc1 - b[0:1]), count, jnp.exp(sc0 - a[0:1]) / z, ok


def _retrieve_kernel(hn_ref, wq_ref, keys_ref, r1_ref, e1_ref, q_ref, c_ref, qs_ref):
    def project(h):
        qh = jnp.dot(hn_ref[...], wq_ref[h], preferred_element_type=F32).astype(qs_ref.dtype)
        qs_ref[2 * h] = qh[:, :PEER_HALF]
        qs_ref[2 * h + 1] = qh[:, PEER_HALF:]

    project(0)

    def head(h, carry):
        def scores(half):
            return lax.dot_general(keys_ref[h, half], qs_ref[2 * h + half],
                                   (((1,), (1,)), ((), ())), preferred_element_type=F32)

        sc0, sc1 = scores(0), scores(1)
        project(jnp.minimum(h + 1, PEER_HEADS - 1))

        def emit(exact):
            rank1, gate1, count, gate0, ok = _retrieve_head(sc0, sc1, exact)
            r1_ref[h] = rank1.astype(r1_ref.dtype)
            e1_ref[h] = gate1.astype(e1_ref.dtype)
            q_ref[h] = count
            c_ref[h] = gate0
            return ok

        ok = emit(False)

        @pl.when(jnp.min(ok) < 0.5)
        def _():
            emit(True)

        return carry

    lax.fori_loop(0, PEER_HEADS, head, 0)


def _retrieve(hn, wq, keys, l, tt):
    s, d = hn.shape
    shape = jax.ShapeDtypeStruct((PEER_HEADS, PEER_KEYS, s), F32)
    shape_lo = jax.ShapeDtypeStruct((PEER_HEADS, PEER_KEYS, s), BF16)
    out_spec = pl.BlockSpec((PEER_HEADS, PEER_KEYS, tt), lambda i: (0, 0, i))
    return pl.pallas_call(
        _retrieve_kernel,
        out_shape=(shape_lo, shape_lo, shape, shape),
        grid=(s // tt,),
        in_specs=[pl.BlockSpec((tt, d), lambda i: (i, 0)),
                  _resident((None,) + wq.shape[1:], lambda i: (l, 0, 0, 0)),
                  _resident((None,) + keys.shape[1:], lambda i: (l, 0, 0, 0, 0))],
        out_specs=(out_spec, out_spec, out_spec, out_spec),
        scratch_shapes=[pltpu.VMEM((2 * PEER_HEADS, tt, PEER_HALF), BF16)],
        compiler_params=_params("parallel"),
        name="peer_retrieve",
    )(hn, wq, keys)


EXPERT_TILE = 4 * PEER_KEYS
TILE_ROWS = EXPERT_TILE // PEER_KEYS
TOKEN_PIECE = 256


def _gate_tile(act, r1_ref, e1_ref, q_ref, c_ref, row0, cols):
    lo = e1_ref.dtype
    zero = jnp.zeros((), lo)
    blocks = []
    for ii in range(TILE_ROWS):
        i = row0 + ii
        w = None
        for h in range(PEER_HEADS):
            qrow = q_ref[h, i:i + 1, cols].astype(lo)
            crow = c_ref[h, i:i + 1, cols].astype(lo)
            gate = jnp.where(r1_ref[h, :, cols] < qrow, e1_ref[h, :, cols] * crow, zero)
            w = gate if w is None else w + gate
        blocks.append(w.astype(F32) * _gelu_tanh(act[ii * PEER_KEYS:(ii + 1) * PEER_KEYS]))
    return jnp.concatenate(blocks, axis=0)


def _experts_kernel(hnt_ref, u_ref, vt_ref, r1_ref, e1_ref, qa_ref, ca_ref, qb_ref, cb_ref, o_ref,
                    act_ref, a_ref):
    g = pl.program_id(1)
    last = pl.num_programs(1) - 1
    even, odd = slice(0, EXPERT_TILE), slice(EXPERT_TILE, 2 * EXPERT_TILE)

    @pl.when(g == 0)
    def _():
        o_ref[...] = jnp.zeros(o_ref.shape, o_ref.dtype)
        a_ref[0] = jnp.zeros(a_ref.shape[1:], a_ref.dtype)
        act_ref[1] = jnp.zeros(act_ref.shape[1:], act_ref.dtype)

    tt = hnt_ref.shape[1]
    pieces = [slice(c, c + TOKEN_PIECE) for c in range(0, tt, TOKEN_PIECE)]

    def first_matmul(slot, rows, tc):
        act_ref[slot, :, tc] = jnp.dot(u_ref[rows], hnt_ref[:, tc], preferred_element_type=F32)

    def gates(slot, q_ref, c_ref, row0, tc):
        a_ref[slot, :, tc] = _gate_tile(act_ref[slot, :, tc], r1_ref, e1_ref, q_ref, c_ref, row0,
                                        tc).astype(a_ref.dtype)

    def second_matmul(slot, cols, tc):
        o_ref[:, tc] += jnp.dot(vt_ref[:, cols], a_ref[slot, :, tc], preferred_element_type=F32)

    @pl.when(g < last)
    def _():
        for tc in pieces:
            second_matmul(0, even, tc)
            gates(1, qa_ref, ca_ref, TILE_ROWS, tc)
            first_matmul(0, even, tc)
        for tc in pieces:
            second_matmul(1, odd, tc)
            gates(0, qb_ref, cb_ref, 0, tc)
            first_matmul(1, odd, tc)

    @pl.when(g == last)
    def _():
        for tc in pieces:
            second_matmul(0, even, tc)
            gates(1, qa_ref, ca_ref, TILE_ROWS, tc)
        for tc in pieces:
            second_matmul(1, odd, tc)


def _experts(hn_t, u_tab, v_tab_t, l, r1, e1, q, c, tt):
    d, s = hn_t.shape
    n_exp = u_tab.shape[1]
    eb = 2 * EXPERT_TILE
    n_blocks = n_exp // eb
    behind = lambda e: jnp.maximum(e - 1, 0)
    ahead = lambda e: jnp.minimum(e, n_blocks - 1)
    sel_spec = pl.BlockSpec((PEER_HEADS, PEER_KEYS, tt), lambda i, e: (0, 0, i), pipeline_mode=pl.Buffered(1))
    row_block = (PEER_HEADS, eb // PEER_KEYS, tt)
    return pl.pallas_call(
        _experts_kernel,
        out_shape=jax.ShapeDtypeStruct((d, s), F32),
        grid=(s // tt, n_blocks + 1),
        in_specs=[pl.BlockSpec((d, tt), lambda i, e: (0, i), pipeline_mode=pl.Buffered(1)),
                  pl.BlockSpec((None, eb, d), lambda i, e: (l, ahead(e), 0)),
                  pl.BlockSpec((None, d, eb), lambda i, e: (l, 0, behind(e))),
                  sel_spec, sel_spec,
                  pl.BlockSpec(row_block, lambda i, e: (0, behind(e), i)),
                  pl.BlockSpec(row_block, lambda i, e: (0, behind(e), i)),
                  pl.BlockSpec(row_block, lambda i, e: (0, ahead(e), i)),
                  pl.BlockSpec(row_block, lambda i, e: (0, ahead(e), i))],
        out_specs=pl.BlockSpec((d, tt), lambda i, e: (0, i)),
        scratch_shapes=[pltpu.VMEM((2, EXPERT_TILE, tt), F32), pltpu.VMEM((2, EXPERT_TILE, tt), BF16)],
        compiler_params=_params("parallel", "arbitrary"),
        name="peer_experts",
    )(hn_t, u_tab, v_tab_t, r1, e1, q, c, q, c)


TOKENS_ROWWISE = 512
TOKENS_MERGE_OUTPROJ = 256
TOKENS_MATMUL = 1024
COLS_INPROJ = 1536
COLS_GATES = 2048
TOKENS_ATTENTION = 512
TOKENS_RETRIEVE = 512
TOKENS_EXPERTS = 1024


def _tile(s, want):
    t = min(s, want)
    assert s % t == 0
    return t


def _layer(h, delta_t, l, w, p):
    s, d = h.shape
    rows = _tile(s, TOKENS_ROWWISE)
    h, xn = _addnorm(h, delta_t, p["norm1_g"], BF16, rows, True)
    zmix, fg = _inproj(xn, w["w_mix"], w["w_fg"], l, _tile(s, TOKENS_MATMUL), COLS_INPROJ)
    gates = _gates(xn, w["w_gate"], l, _tile(s, TOKENS_MATMUL), COLS_GATES)

    f_t = jnp.transpose(fg[:, :8])
    f_cum = _fcum(f_t, p["forget_b"])
    f_rows = f_cum[:GROUPS].reshape(GROUPS, 1, s)
    od = _attention(zmix, f_rows, _tile(s, TOKENS_ATTENTION), _tile(s, TOKENS_ATTENTION))

    abc = _mixers(zmix, p["conv_w"], p["sgu_norm_g"], p["sgu_w"], p["sgu_b"], p["pool_w"],
                  p["pool_scale"], rows)
    h1, hn, hn_t = _merge_outproj(abc, od, gates, w["w_branch"], w["w_out"], l, h, p["norm2_g"],
                                  _tile(s, TOKENS_MERGE_OUTPROJ))

    r1, e1, q, c = _retrieve(hn, w["peer_wq"], w["peer_keys"], l, _tile(s, TOKENS_RETRIEVE))
    return h1, _experts(hn_t, w["peer_u"], w["peer_v_t"], l, r1, e1, q, c, _tile(s, TOKENS_EXPERTS))


def kernel(x, norm1_g, w_in, conv_w, sgu_norm_g, sgu_w, sgu_b, pool_w, pool_scale, forget_b,
           w_branch, w_out, norm2_g, peer_wq, peer_keys, peer_u, peer_v, final_g):
    bsz, s, d = x.shape
    depth = w_in.shape[0]
    off_g = N_MIX + GROUPS
    assert w_in.shape[2] == off_g + N_BRANCH * d
    tril = jnp.tril(jnp.ones((CHUNK, CHUNK), dtype=bool))

    def columns_t(lo, hi):
        return jnp.transpose(w_in[:, :, lo:hi].astype(BF16), (2, 0, 1)).reshape(hi - lo, depth * d)

    w = {
        "w_mix": columns_t(0, N_MIX),
        "w_fg": jnp.pad(columns_t(N_MIX, off_g), ((0, FG_PAD - GROUPS), (0, 0))),
        "w_gate": columns_t(off_g, w_in.shape[2]),
        "w_branch": w_branch.astype(BF16),
        "w_out": w_out.astype(BF16),
        "peer_wq": jnp.transpose(
            peer_wq.astype(BF16).reshape(depth, d, PEER_HEADS, 2 * PEER_HALF), (0, 2, 1, 3)),
        "peer_keys": peer_keys.astype(BF16),
        "peer_u": peer_u.astype(BF16),
        "peer_v_t": jnp.transpose(peer_v.astype(BF16), (0, 2, 1)),
    }

    outs = []
    for b in range(bsz):
        h, delta = x[b], None
        for l in range(depth):
            p = {
                "norm1_g": norm1_g[l][None, :],
                "conv_w": conv_w[l],
                "sgu_norm_g": sgu_norm_g[l][None, :],
                "sgu_w": jnp.where(tril[None], sgu_w[l], 0.0).astype(BF16),
                "sgu_b": jnp.broadcast_to(sgu_b[l][:, :, None], (GROUPS, CHUNK, HEAD_DIM)),
                "pool_w": pool_w[l].astype(BF16),
                "pool_scale": pool_scale[l][None, :],
                "forget_b": jnp.pad(forget_b[l], (0, 8 - GROUPS))[:, None],
                "norm2_g": norm2_g[l][None, :],
            }
            h, delta = _layer(h, delta, l, w, p)
        outs.append(_addnorm(h, delta, final_g[None, :], F32, _tile(s, TOKENS_ROWWISE), False)[1])
    return jnp.stack(outs, axis=0)
```
